```python
import math
import jax, jax.numpy as jnp
from jax import lax
import numpy as np

D_MODEL = 1024
BATCH = 1
SEQ = 16384
DEPTH = 1
DEC_BATCH = 128
DEC_SEQ = 4
PAST_LEN = 8192
PAGE_SIZE = 128

HEAD_DIM = 64
NSA_HEADS = 8
NSA_KV_GROUPS = 2
HEADS_PER_GROUP = NSA_HEADS // NSA_KV_GROUPS
SB_HEADS = 8
NSA_WIDTH = NSA_HEADS * HEAD_DIM
SB_WIDTH = SB_HEADS * HEAD_DIM
KV_WIDTH = NSA_KV_GROUPS * HEAD_DIM
CMP_BLOCK = 32
CMP_STRIDE = 16
SEL_BLOCK = 64
SEL_TOP = 16
SEL_COVER_W = (0.5, 1.0, 1.0, 1.0, 0.5)
WINDOW = 512
N_BUCKETS = 32
MAX_DISTANCE = 128
D_FF = 2816
CONV_W = 3
QUERY_BLOCK = 128
IN_COLS = NSA_WIDTH + 6 * KV_WIDTH + 3 * NSA_HEADS + 3 * SB_WIDTH + 2 * D_MODEL
SCALE = HEAD_DIM ** -0.5
EPS = 1e-6
NEG = -1e30
FORCE_SCORE = 1e9

kernel_name = 'nsa_stickbreak_convffn_step'


def _rmsnorm(x, g):
    xf = x.astype(jnp.float32)
    y = xf * lax.rsqrt(jnp.mean(xf * xf, axis=-1, keepdims=True) + EPS)
    return (y * g.astype(jnp.float32)).astype(x.dtype)


def _split_projection(h, w_in):
    sizes = [NSA_WIDTH] + [KV_WIDTH] * 6 + [3 * NSA_HEADS] + [SB_WIDTH] * 3 + [D_MODEL, D_MODEL]
    cuts = np.cumsum(sizes)[:-1].tolist()
    return jnp.split(h @ w_in, cuts, axis=-1)


def _t5_bucket(dist):
    n = jnp.maximum(dist, 0)
    max_exact = N_BUCKETS // 2
    nf = jnp.maximum(n, 1).astype(jnp.float32)
    large = max_exact + (jnp.log(nf / max_exact) / math.log(MAX_DISTANCE / max_exact)
                         * (N_BUCKETS - max_exact)).astype(jnp.int32)
    return jnp.where(n < max_exact, n, jnp.minimum(large, N_BUCKETS - 1))


def _masked_softmax(s, mask):
    p = jax.nn.softmax(jnp.where(mask, s, NEG), axis=-1)
    return jnp.where(mask, p, 0.0)


def _compress_chunks(rows, w1):
    b, L, g, dh = rows.shape
    c = rows.reshape(b, L // CMP_STRIDE, CMP_STRIDE, g, dh)
    lo = jnp.einsum('bclgd,lde->bcge', c, w1[:CMP_STRIDE])
    hi = jnp.einsum('bclgd,lde->bcge', c, w1[CMP_STRIDE:])
    return lo, hi


def _compress_finish(lo, hi, n_cmp, b1, w2):
    hid = lo[:, :n_cmp] + hi[:, 1:n_cmp + 1] + b1
    return jnp.einsum('bnge,ef->bngf', jax.nn.gelu(hid), w2)


def _cmp_branch(q, kc, vc, t, bias_g):
    n = kc.shape[1]
    end = CMP_STRIDE * jnp.arange(n) + CMP_BLOCK - 1
    dist = t[:, None] - end[None, :]
    bias = jnp.transpose(bias_g[_t5_bucket(dist)], (2, 3, 0, 1)).astype(jnp.float32)
    s = jnp.einsum('bqghd,bngd->bghqn', q, kc).astype(jnp.float32) * SCALE + bias
    p = _masked_softmax(s, dist >= 0)
    o = jnp.einsum('bghqn,bngd->bqghd', p.astype(vc.dtype), vc)
    return o, p.sum(axis=2)


def _select_blocks(p_grp, t, n_sel):
    n_cmp = p_grp.shape[-1]
    ratio = SEL_BLOCK // CMP_STRIDE
    pad_r = max(0, ratio * n_sel - n_cmp)
    padded = jnp.pad(p_grp, ((0, 0), (0, 0), (0, 0), (1, pad_r)))
    idx = ratio * np.arange(n_sel)[:, None] + np.arange(ratio + 1)[None, :]
    p_slc = jnp.einsum('bgqjk,k->bgqj', padded[..., idx], jnp.asarray(SEL_COVER_W, jnp.float32))
    blk = jnp.arange(n_sel)[None, :]
    cur = (t // SEL_BLOCK)[:, None]
    valid = blk * SEL_BLOCK <= t[:, None]
    forced = (blk == 0) | (blk == cur) | (blk == cur - 1)
    score = jnp.where(forced, FORCE_SCORE, jnp.where(valid, p_slc, NEG))
    _, top = lax.top_k(score, min(SEL_TOP, n_sel))
    return top


def _sel_branch(q, ks, vs, kpos, t, bias_g):
    g = ks.shape[1]
    dist = t[None, None, :, None] - kpos
    garr = jnp.arange(g)[None, :, None, None]
    bias = jnp.moveaxis(bias_g[_t5_bucket(dist), garr], -1, 2).astype(jnp.float32)
    s = jnp.einsum('bqghd,bgqkd->bghqk', q, ks).astype(jnp.float32) * SCALE + bias
    p = _masked_softmax(s, (dist >= 0)[:, :, None])
    return jnp.einsum('bghqk,bgqkd->bqghd', p.astype(vs.dtype), vs)


def _win_branch(q, kw, vw, kpos, t, bias_g):
    dist = t[:, None] - kpos[None, :]
    mask = (dist >= 0) & (dist < WINDOW) & (kpos[None, :] >= 0)
    bias = jnp.transpose(bias_g[_t5_bucket(dist)], (2, 3, 0, 1)).astype(jnp.float32)
    s = jnp.einsum('bqghd,bkgd->bghqk', q, kw).astype(jnp.float32) * SCALE + bias
    p = _masked_softmax(s, mask)
    return jnp.einsum('bghqk,bkgd->bqghd', p.astype(vw.dtype), vw)


def _nsa_combine(gate_logits, o_c, o_s, o_w):
    b, q, _ = gate_logits.shape
    g = jax.nn.sigmoid(gate_logits).reshape(b, q, 3, NSA_KV_GROUPS, HEADS_PER_GROUP, 1)
    o = g[:, :, 0] * o_c + g[:, :, 1] * o_s + g[:, :, 2] * o_w
    return o.reshape(b, q, NSA_WIDTH)


def _excl_rev_cumsum(x):
    c = lax.cumsum(x, axis=x.ndim - 1, reverse=True)
    return jnp.concatenate([c[..., 1:], jnp.zeros_like(c[..., :1])], axis=-1)


def _sb_weights(z, mask, carry):
    l1 = jnp.where(mask, jax.nn.log_sigmoid(-z), 0.0)
    log_a = jax.nn.log_sigmoid(z) + _excl_rev_cumsum(l1) + carry[..., None]
    return jnp.where(mask, jnp.exp(log_a), 0.0), l1.sum(axis=-1)


def _sb_prompt(q, k, v):
    b, T, H, dh = q.shape
    kpos = jnp.arange(T)

    def block(c):
        qb = lax.dynamic_slice_in_dim(q, c, QUERY_BLOCK, axis=1)
        t = c + jnp.arange(QUERY_BLOCK)
        z = jnp.einsum('bqhd,bkhd->bhqk', qb, k).astype(jnp.float32) * SCALE
        a, _ = _sb_weights(z, kpos[None, :] < t[:, None], jnp.zeros((), jnp.float32))
        return jnp.einsum('bhqk,bkhd->bqhd', a.astype(v.dtype), v)

    out = lax.map(block, jnp.arange(T // QUERY_BLOCK) * QUERY_BLOCK)
    return jnp.moveaxis(out, 0, 1).reshape(b, T, H * dh)


def _sb_sample(q, k_new, v_new, cache_sb, l, page_table):
    db, Q, H, dh = q.shape
    qi = jnp.arange(Q)
    z = jnp.einsum('bqhd,bkhd->bhqk', q, k_new).astype(jnp.float32) * SCALE
    a, lsum = _sb_weights(z, qi[None, :] < qi[:, None], jnp.zeros((), jnp.float32))
    o = jnp.einsum('bhqk,bkhd->bqhd', a, v_new.astype(jnp.float32))

    def step(carry, p):
        o, lsum = carry
        kv = cache_sb[l, page_table[:, p]]
        z = jnp.einsum('bqhd,bkhd->bhqk', q, kv[:, :, 0]).astype(jnp.float32) * SCALE
        a, part = _sb_weights(z, True, lsum)
        o = o + jnp.einsum('bhqk,bkhd->bqhd', a, kv[:, :, 1].astype(jnp.float32))
        return (o, lsum + part), None

    (o, _), _ = lax.scan(step, (o, lsum), jnp.arange(page_table.shape[1])[::-1])
    return o.astype(q.dtype).reshape(db, Q, H * dh)


def _merge_ffn(x, o_a, o_b, g_ma, g_mb, conv_buf, lw):
    mixed = (jax.nn.sigmoid(g_ma) * (o_a @ lw['w_branch_a'])
             + jax.nn.sigmoid(g_mb) * (o_b @ lw['w_branch_b']))
    x1 = x + _rmsnorm(mixed @ lw['w_out'], lw['g_post_mix'])
    h2 = _rmsnorm(x1, lw['g_pre_ffn'])
    a, b = jnp.split(h2 @ lw['w_ffn_up'], 2, axis=-1)
    T = x.shape[1]
    a_full = jnp.concatenate([conv_buf.astype(a.dtype), a], axis=1)
    conv = lw['conv_b']
    for j in range(CONV_W):
        conv = conv + lw['conv_w'][j] * a_full[:, j:j + T]
    f = jax.nn.gelu(conv) * b
    x2 = x1 + _rmsnorm(f @ lw['w_ffn_down'], lw['g_post_ffn'])
    return x2, a_full[:, T:]


def _prompt_layer(x, lw, bias_g):
    B, T, _ = x.shape
    G, HPG, dh = NSA_KV_GROUPS, HEADS_PER_GROUP, HEAD_DIM
    h = _rmsnorm(x, lw['g_pre_mix'])
    (q_a, k_c, v_c, k_s, v_s, k_w, v_w, g_nsa, q_b, k_b, v_b, g_ma, g_mb) = _split_projection(h, lw['w_in'])
    q_a = q_a.reshape(B, T, G, HPG, dh)
    k_c, v_c, k_s, v_s, k_w, v_w = [a.reshape(B, T, G, dh) for a in (k_c, v_c, k_s, v_s, k_w, v_w)]
    q_b, k_b, v_b = [a.reshape(B, T, SB_HEADS, dh) for a in (q_b, k_b, v_b)]

    n_cmp = (T - CMP_BLOCK) // CMP_STRIDE + 1
    kc = _compress_finish(*_compress_chunks(k_c, lw['w_ck1']), n_cmp, lw['b_ck1'], lw['w_ck2'])
    vc = _compress_finish(*_compress_chunks(v_c, lw['w_cv1']), n_cmp, lw['b_cv1'], lw['w_cv2'])
    n_sel = T // SEL_BLOCK
    ks_t = jnp.moveaxis(k_s, 1, 2)
    vs_t = jnp.moveaxis(v_s, 1, 2)
    kw_pad = jnp.pad(k_w, ((0, 0), (WINDOW, 0), (0, 0), (0, 0)))
    vw_pad = jnp.pad(v_w, ((0, 0), (WINDOW, 0), (0, 0), (0, 0)))
    barr = jnp.arange(B)[:, None, None, None]
    garr = jnp.arange(G)[None, :, None, None]

    def nsa_block(c):
        qb = lax.dynamic_slice_in_dim(q_a, c, QUERY_BLOCK, axis=1)
        gb = lax.dynamic_slice_in_dim(g_nsa, c, QUERY_BLOCK, axis=1)
        t = c + jnp.arange(QUERY_BLOCK)
        o_c, p_grp = _cmp_branch(qb, kc, vc, t, bias_g)
        top = _select_blocks(p_grp, t, n_sel)
        pos = (top[..., None] * SEL_BLOCK + jnp.arange(SEL_BLOCK)).reshape(B, G, QUERY_BLOCK, -1)
        o_s = _sel_branch(qb, ks_t[barr, garr, pos], vs_t[barr, garr, pos], pos, t, bias_g)
        kpos = c - WINDOW + jnp.arange(QUERY_BLOCK + WINDOW)
        kwb = lax.dynamic_slice_in_dim(kw_pad, c, QUERY_BLOCK + WINDOW, axis=1)
        vwb = lax.dynamic_slice_in_dim(vw_pad, c, QUERY_BLOCK + WINDOW, axis=1)
        o_w = _win_branch(qb, kwb, vwb, kpos, t, bias_g)
        return _nsa_combine(gb, o_c, o_s, o_w)

    o_a = lax.map(nsa_block, jnp.arange(T // QUERY_BLOCK) * QUERY_BLOCK)
    o_a = jnp.moveaxis(o_a, 0, 1).reshape(B, T, NSA_WIDTH)
    o_b = _sb_prompt(q_b, k_b, v_b)
    y, conv_state = _merge_ffn(x, o_a, o_b, g_ma, g_mb, jnp.zeros((B, CONV_W - 1, D_FF), x.dtype), lw)
    w_buf = min(WINDOW, PAST_LEN)
    new = (jnp.stack([k_c, v_c], axis=2), jnp.stack([k_s, v_s], axis=2), jnp.stack([k_b, v_b], axis=2),
           jnp.stack([kw_pad, vw_pad], axis=2)[:, -w_buf:], conv_state)
    return y, new


def _sample_layer(x, l, cache_cmp, cache_sel, cache_sb, state_win, state_conv, page_table, lw, bias_g):
    DB, Q, _ = x.shape
    G, HPG, dh = NSA_KV_GROUPS, HEADS_PER_GROUP, HEAD_DIM
    n_pages = page_table.shape[1]
    h = _rmsnorm(x, lw['g_pre_mix'])
    (q_a, k_c, v_c, k_s, v_s, k_w, v_w, g_nsa, q_b, k_b, v_b, g_ma, g_mb) = _split_projection(h, lw['w_in'])
    q_a = q_a.reshape(DB, Q, G, HPG, dh)
    k_c, v_c, k_s, v_s, k_w, v_w = [a.reshape(DB, Q, G, dh) for a in (k_c, v_c, k_s, v_s, k_w, v_w)]
    q_b, k_b, v_b = [a.reshape(DB, Q, SB_HEADS, dh) for a in (q_b, k_b, v_b)]
    t = PAST_LEN + jnp.arange(Q)

    def page_partials(p):
        rows = cache_cmp[l, page_table[:, p]]
        return _compress_chunks(rows[:, :, 0], lw['w_ck1']) + _compress_chunks(rows[:, :, 1], lw['w_cv1'])

    past_parts = lax.map(page_partials, jnp.arange(n_pages))
    q_pad = -(-Q // CMP_STRIDE) * CMP_STRIDE
    pad_new = lambda a: jnp.pad(a, ((0, 0), (0, q_pad - Q), (0, 0), (0, 0)))
    new_parts = _compress_chunks(pad_new(k_c), lw['w_ck1']) + _compress_chunks(pad_new(v_c), lw['w_cv1'])
    n_chunk_past = n_pages * (PAGE_SIZE // CMP_STRIDE)
    k_lo, k_hi, v_lo, v_hi = [
        jnp.concatenate([jnp.moveaxis(pa, 0, 1).reshape(DB, n_chunk_past, G, dh), na], axis=1)
        for pa, na in zip(past_parts, new_parts)]
    n_cmp = (PAST_LEN + Q - CMP_BLOCK) // CMP_STRIDE + 1
    kc = _compress_finish(k_lo, k_hi, n_cmp, lw['b_ck1'], lw['w_ck2'])
    vc = _compress_finish(v_lo, v_hi, n_cmp, lw['b_cv1'], lw['w_cv2'])
    o_c, p_grp = _cmp_branch(q_a, kc, vc, t, bias_g)

    n_sel = -(-(PAST_LEN + Q) // SEL_BLOCK)
    top = _select_blocks(p_grp, t, n_sel)
    pos = (top[..., None] * SEL_BLOCK + jnp.arange(SEL_BLOCK)).reshape(DB, G, Q, -1)
    barr = jnp.arange(DB)[:, None, None, None]
    garr = jnp.arange(G)[None, :, None, None]
    in_past = (pos < PAST_LEN)[..., None]
    ppos = jnp.minimum(pos, PAST_LEN - 1)
    phys = page_table[barr, ppos // PAGE_SIZE]
    off = ppos % PAGE_SIZE
    npos = jnp.clip(pos - PAST_LEN, 0, Q - 1)
    ks = jnp.where(in_past, cache_sel[l, phys, off, 0, garr], k_s[barr, npos, garr])
    vs = jnp.where(in_past, cache_sel[l, phys, off, 1, garr], v_s[barr, npos, garr])
    o_s = _sel_branch(q_a, ks, vs, pos, t, bias_g)

    w_buf = state_win.shape[2]
    kw = jnp.concatenate([state_win[l, :, :, 0].astype(k_w.dtype), k_w], axis=1)
    vw = jnp.concatenate([state_win[l, :, :, 1].astype(v_w.dtype), v_w], axis=1)
    kpos = PAST_LEN - w_buf + jnp.arange(w_buf + Q)
    o_w = _win_branch(q_a, kw, vw, kpos, t, bias_g)
    o_a = _nsa_combine(g_nsa, o_c, o_s, o_w)

    o_b = _sb_sample(q_b, k_b, v_b, cache_sb, l, page_table)
    y, conv_state = _merge_ffn(x, o_a, o_b, g_ma, g_mb, state_conv[l], lw)
    new = (jnp.stack([k_c, v_c], axis=2), jnp.stack([k_s, v_s], axis=2), jnp.stack([k_b, v_b], axis=2),
           jnp.stack([kw, vw], axis=2)[:, -w_buf:], conv_state)
    return y, new


def _normal_pool(key, n_phys, row_shape):
    keys = jax.random.split(key, DEPTH * n_phys).reshape(DEPTH, n_phys)
    return lax.map(lambda kd: lax.map(lambda k: jax.random.normal(k, row_shape, jnp.float32), kd), keys)


def setup_inputs(seed: int = 0) -> dict:
    key = jax.random.key(seed)
    keys = list(jax.random.split(key, 40))
    n_pages = PAST_LEN // PAGE_SIZE
    n_used = DEC_BATCH * n_pages
    n_phys = n_used + n_used // 4
    w_buf = min(WINDOW, PAST_LEN)
    G, dh = NSA_KV_GROUPS, HEAD_DIM

    def nrm(shape, scale=1.0):
        return scale * jax.random.normal(keys.pop(), shape, jnp.float32)

    def gain():
        return 1.0 + 0.05 * nrm((DEPTH, D_MODEL))

    page_table = jax.random.permutation(keys.pop(), n_phys)[:n_used].reshape(DEC_BATCH, n_pages).astype(jnp.int32)
    return {
        'x_prompt': nrm((BATCH, SEQ, D_MODEL)),
        'x_sample': nrm((DEC_BATCH, DEC_SEQ, D_MODEL)),
        'cache_cmp': _normal_pool(keys.pop(), n_phys, (PAGE_SIZE, 2, G, dh)),
        'cache_sel': _normal_pool(keys.pop(), n_phys, (PAGE_SIZE, 2, G, dh)),
        'cache_sb': _normal_pool(keys.pop(), n_phys, (PAGE_SIZE, 2, SB_HEADS, dh)),
        'state_win': nrm((DEPTH, DEC_BATCH, w_buf, 2, G, dh)),
        'state_conv': nrm((DEPTH, DEC_BATCH, CONV_W - 1, D_FF)),
        'page_table': page_table,
        'rel_bias': nrm((N_BUCKETS, NSA_HEADS), 0.5),
        'g_pre_mix': gain(),
        'w_in': nrm((DEPTH, D_MODEL, IN_COLS), D_MODEL ** -0.5),
        'w_ck1': nrm((DEPTH, CMP_BLOCK, dh, dh), (CMP_BLOCK * dh) ** -0.5),
        'b_ck1': nrm((DEPTH, dh), 0.02),
        'w_ck2': nrm((DEPTH, dh, dh), dh ** -0.5),
        'w_cv1': nrm((DEPTH, CMP_BLOCK, dh, dh), (CMP_BLOCK * dh) ** -0.5),
        'b_cv1': nrm((DEPTH, dh), 0.02),
        'w_cv2': nrm((DEPTH, dh, dh), dh ** -0.5),
        'w_branch_a': nrm((DEPTH, NSA_WIDTH, D_MODEL), NSA_WIDTH ** -0.5),
        'w_branch_b': nrm((DEPTH, SB_WIDTH, D_MODEL), SB_WIDTH ** -0.5),
        'w_out': nrm((DEPTH, D_MODEL, D_MODEL), D_MODEL ** -0.5),
        'g_post_mix': gain(),
        'g_pre_ffn': gain(),
        'w_ffn_up': nrm((DEPTH, D_MODEL, 2 * D_FF), D_MODEL ** -0.5),
        'conv_w': nrm((DEPTH, CONV_W, D_FF), CONV_W ** -0.5),
        'conv_b': nrm((DEPTH, D_FF), 0.02),
        'w_ffn_down': nrm((DEPTH, D_FF, D_MODEL), D_FF ** -0.5),
        'g_post_ffn': gain(),
    }


def reference(x_prompt, x_sample, cache_cmp, cache_sel, cache_sb, state_win, state_conv, page_table,
              rel_bias, g_pre_mix, w_in, w_ck1, b_ck1, w_ck2, w_cv1, b_cv1, w_cv2, w_branch_a, w_branch_b,
              w_out, g_post_mix, g_pre_ffn, w_ffn_up, conv_w, conv_b, w_ffn_down, g_post_ffn):
    bias_g = rel_bias.reshape(N_BUCKETS, NSA_KV_GROUPS, HEADS_PER_GROUP)
    y_p, y_s = x_prompt, x_sample
    st_p, st_s = [], []
    for l in range(DEPTH):
        lw = {'g_pre_mix': g_pre_mix[l], 'w_in': w_in[l], 'w_ck1': w_ck1[l], 'b_ck1': b_ck1[l],
              'w_ck2': w_ck2[l], 'w_cv1': w_cv1[l], 'b_cv1': b_cv1[l], 'w_cv2': w_cv2[l],
              'w_branch_a': w_branch_a[l], 'w_branch_b': w_branch_b[l], 'w_out': w_out[l],
              'g_post_mix': g_post_mix[l], 'g_pre_ffn': g_pre_ffn[l], 'w_ffn_up': w_ffn_up[l],
              'conv_w': conv_w[l], 'conv_b': conv_b[l], 'w_ffn_down': w_ffn_down[l], 'g_post_ffn': g_post_ffn[l]}
        y_p, sp = _prompt_layer(y_p, lw, bias_g)
        y_s, ss = _sample_layer(y_s, l, cache_cmp, cache_sel, cache_sb, state_win, state_conv, page_table, lw, bias_g)
        st_p.append(sp)
        st_s.append(ss)
    kv_cmp_p, kv_sel_p, kv_sb_p, win_p, conv_p = [jnp.stack([s[i] for s in st_p]) for i in range(5)]
    kv_cmp_s, kv_sel_s, kv_sb_s, win_s, conv_s = [jnp.stack([s[i] for s in st_s]) for i in range(5)]
    return (y_p, y_s, kv_cmp_p, kv_sel_p, kv_sb_p, win_p, conv_p, kv_cmp_s, kv_sel_s, kv_sb_s, win_s, conv_s)
```

```python
import functools
import math

import numpy as np
import jax
import jax.numpy as jnp
from jax import lax
from jax.experimental import pallas as pl
from jax.experimental.pallas import tpu as pltpu

D_MODEL = 1024
SEQ = 16384
DEPTH = 1
DEC_BATCH = 128
DEC_SEQ = 4
PAST_LEN = 8192
PAGE_SIZE = 128
HEAD_DIM = 64
NSA_HEADS = 8
NSA_KV_GROUPS = 2
HEADS_PER_GROUP = NSA_HEADS // NSA_KV_GROUPS
SB_HEADS = 8
NSA_WIDTH = NSA_HEADS * HEAD_DIM
SB_WIDTH = SB_HEADS * HEAD_DIM
KV_WIDTH = NSA_KV_GROUPS * HEAD_DIM
CMP_BLOCK = 32
CMP_STRIDE = 16
SEL_BLOCK = 64
SEL_TOP = 16
SEL_COVER_W = (0.5, 1.0, 1.0, 1.0, 0.5)
WINDOW = 512
N_BUCKETS = 32
MAX_DISTANCE = 128
D_FF = 2816
CONV_W = 3
SCALE = HEAD_DIM ** -0.5
EPS = 1e-6
NEG = -1e30
FORCE_SCORE = 1e9
MASKED_BELOW = -5e29
LANES = 128
SUBLANES = 8
QUERY_TILE = 128
KEY_TILE = 128
ROW_TILE = 256
CMP_PAD = 16
CMP_WIN = 24
PAGES_PER_STEP = 16
VMEM_LIMIT = 56 * 1024 * 1024

_F32 = jnp.float32
_BF16 = jnp.bfloat16


def _bf(x):
    return x.astype(_BF16)


def _dot(a, b):
    return jnp.dot(a, b, preferred_element_type=_F32)


def _dot_nt(a, b):
    return lax.dot_general(a, b, (((1,), (1,)), ((), ())), preferred_element_type=_F32)


def _dot_tn(a, b):
    return lax.dot_general(a, b, (((0,), (0,)), ((), ())), preferred_element_type=_F32)


def _split_hi_lo(x):
    hi = _bf(x)
    lo = _bf(x - hi.astype(_F32))
    return hi, lo


def _rms(x, g):
    return x * lax.rsqrt(jnp.mean(x * x, axis=-1, keepdims=True) + EPS) * g


def _softplus(z):
    return jnp.maximum(z, 0.0) + jnp.log(1.0 + jnp.exp(-jnp.abs(z)))


def _const_spec(shape):
    nd = len(shape)
    return pl.BlockSpec(shape, lambda *_: (0,) * nd)


def _params(sem=None):
    return pltpu.CompilerParams(dimension_semantics=sem, vmem_limit_bytes=VMEM_LIMIT)


def _bucket_np(dist):
    n = np.maximum(dist, 0)
    max_exact = N_BUCKETS // 2
    nf = np.maximum(n, 1).astype(np.float32)
    large = max_exact + (np.log(nf / max_exact) / math.log(MAX_DISTANCE / max_exact)
                         * (N_BUCKETS - max_exact)).astype(np.int32)
    return np.where(n < max_exact, n, np.minimum(large, N_BUCKETS - 1)).astype(np.int32)


def _bias_table(rel_bias, dist, valid, heads):
    b = rel_bias[_bucket_np(dist), heads]
    return jnp.where(valid, b, NEG).astype(_F32)


def _cover_matrix(n_rows, n_cols, n_cmp, row_offset):
    r = np.arange(n_rows)[:, None] - row_offset
    j = np.arange(n_cols)[None, :]
    k = r - 4 * j + 1
    w = np.asarray(SEL_COVER_W, np.float32)
    ok = (k >= 0) & (k <= 4) & (r >= 0) & (r < n_cmp)
    return np.where(ok, w[np.clip(k, 0, 4)], 0.0).astype(np.float32)


def _cumsum_matrix():
    k = np.arange(LANES)
    tri = (k[:, None] >= k[None, :]).astype(np.float32)
    half = np.concatenate([tri, np.ones((LANES, LANES), np.float32)], axis=1)
    return np.concatenate([half, half], axis=0)


def _proj_in_kernel(x_ref, g_ref, wqa, wkv, wgn, wqb, wsb, wgm,
                    qa_o, kv_o, kvb_o, gn_o, qb_o, sb_o, sbb_o, gm_o):
    h = _bf(_rms(x_ref[...], g_ref[...]))
    qa_o[...] = _bf(_dot(h, wqa[...]) * SCALE)
    kv = _dot(h, wkv[...])
    kv_o[...] = kv
    kvb_o[...] = _bf(kv)
    gn_o[...] = _dot(h, wgn[...])
    qb_o[...] = _bf(_dot(h, wqb[...]) * SCALE)
    sb = _dot(h, wsb[...])
    sb_o[...] = sb
    sbb_o[...] = _bf(sb)
    gm_o[...] = _dot(h, wgm[...])


def _proj_in(x, g, ws):
    rows = x.shape[0]
    tm = min(ROW_TILE, rows)
    widths = [w.shape[1] for w in ws]
    out_shape = [
        jax.ShapeDtypeStruct((rows, widths[0]), _BF16),
        jax.ShapeDtypeStruct((rows, widths[1]), _F32),
        jax.ShapeDtypeStruct((rows, widths[1]), _BF16),
        jax.ShapeDtypeStruct((rows, widths[2]), _F32),
        jax.ShapeDtypeStruct((rows, widths[3]), _BF16),
        jax.ShapeDtypeStruct((rows, widths[4]), _F32),
        jax.ShapeDtypeStruct((rows, widths[4]), _BF16),
        jax.ShapeDtypeStruct((rows, widths[5]), _F32),
    ]
    row_spec = lambda w: pl.BlockSpec((tm, w), lambda i: (i, 0))
    return pl.pallas_call(
        _proj_in_kernel,
        grid=(rows // tm,),
        in_specs=[row_spec(D_MODEL), _const_spec((1, D_MODEL))] + [_const_spec(w.shape) for w in ws],
        out_specs=[row_spec(s.shape[1]) for s in out_shape],
        out_shape=out_shape,
        compiler_params=_params(("arbitrary",)),
        name="proj_in",
    )(x, g, *ws)


def _compress_finish(parts_lo, parts_hi, b1, w2):
    n = parts_hi.shape[0]
    hi_next = pltpu.roll(parts_hi, n - 1, 0)
    hid = parts_lo + hi_next + b1
    return _dot(_bf(jax.nn.gelu(hid)), w2)


def _compress_kernel(ck_ref, cv_ref, w1k, w1v, b1k, b1v, w2k, w2v, kc_o, vc_o):
    for c_ref, w1, b1, w2, o in ((ck_ref, w1k, b1k, w2k, kc_o), (cv_ref, w1v, b1v, w2v, vc_o)):
        parts = _dot(c_ref[...], w1[...])
        o[...] = _bf(_compress_finish(parts[:, :KV_WIDTH], parts[:, KV_WIDTH:], b1[...], w2[...]))


def _compress_prompt(ck, cv, w1k, w1v, b1k, b1v, w2k, w2v):
    n = ck.shape[0]
    return pl.pallas_call(
        _compress_kernel,
        out_shape=[jax.ShapeDtypeStruct((n, KV_WIDTH), _BF16)] * 2,
        compiler_params=_params(),
        name="compress_prompt",
    )(ck, cv, w1k, w1v, b1k, b1v, w2k, w2v)


def _chunk_weights(w1):
    G, dh = NSA_KV_GROUPS, HEAD_DIM
    w1r = w1.reshape(2, CMP_STRIDE, dh, dh)
    eye = jnp.eye(G, dtype=w1.dtype)
    big = jnp.einsum('slde,gh->lgdshe', w1r, eye)
    return big.reshape(CMP_STRIDE * G * dh, 2 * G * dh)


def _group_diag(w2):
    return jnp.kron(jnp.eye(NSA_KV_GROUPS, dtype=w2.dtype), w2)


def _nsa_prompt_kernel(qa_ref, gn_ref, kcp_ref, vcp_ref, ks_ref, vs_ref, kw_ref, vw_ref,
                       bc_ref, bs_ref, bw_ref, bfar_ref, ct_ref, oa_ref, s_scr, sel_scr):
    G, HPG, dh = NSA_KV_GROUPS, HEADS_PER_GROUP, HEAD_DIM
    QT = QUERY_TILE
    NL = HPG * QT
    qb = pl.program_id(0)
    c = qb * QT
    n_cp = kcp_ref.shape[0]
    n_blk = ct_ref.shape[0]

    lane_half = lax.broadcasted_iota(jnp.int32, (1, LANES), 1) // dh
    gates_t = jnp.transpose(jax.nn.sigmoid(gn_ref[...]))

    def gate_row(branch, g):
        base = branch * NSA_HEADS + g * HPG
        return jnp.concatenate([gates_t[base + h:base + h + 1, :] for h in range(HPG)], axis=1)

    ti = c + lax.broadcasted_iota(jnp.int32, (1, QT), 1)
    cur = ti // SEL_BLOCK

    for g in range(G):
        q_rows = []
        for h in range(HPG):
            col = (g * HPG + h) * dh
            tile = qa_ref[:, (col // LANES) * LANES:(col // LANES + 1) * LANES].astype(_F32)
            if (col % LANES) // dh != g:
                tile = pltpu.roll(tile, dh, 1)
            q_rows.append(_bf(jnp.where(lane_half == g, tile, 0.0)))
        qg = jnp.concatenate(q_rows, axis=0)
        bfar = bfar_ref[g]

        r0 = pl.multiple_of(qb * (QT // CMP_STRIDE), SUBLANES)
        s_scr[...] = _dot_nt(kcp_ref[...], qg) + bfar
        s_scr[pl.ds(r0, CMP_WIN), :] = s_scr[pl.ds(r0, CMP_WIN), :] - bfar + bc_ref[g]
        row = lax.broadcasted_iota(jnp.int32, (n_cp, 1), 0)
        s = jnp.where((row >= CMP_PAD) & (row < r0 + CMP_WIN), s_scr[...], NEG)
        m = jnp.max(s, axis=0, keepdims=True)
        e = jnp.exp(s - m)
        p = jnp.where(s > MASKED_BELOW, e / jnp.sum(e, axis=0, keepdims=True), 0.0)
        o_cmp = _dot_tn(vcp_ref[...], _bf(p))[g * dh:(g + 1) * dh]
        p_grp = p[:, 0:QT]
        for h in range(1, HPG):
            p_grp = p_grp + p[:, h * QT:(h + 1) * QT]
        p_hi, p_lo = _split_hi_lo(p_grp)
        p_slc = _dot(ct_ref[...], p_hi) + _dot(ct_ref[...], p_lo)

        blk = lax.broadcasted_iota(jnp.int32, (n_blk, 1), 0)
        blk_f = blk.astype(_F32)
        forced = (blk == 0) | (blk == cur) | (blk == cur - 1)
        score = jnp.where(forced, FORCE_SCORE, jnp.where(blk <= cur, p_slc, NEG))
        sel = jnp.zeros((n_blk, QT), _F32)
        for _ in range(min(SEL_TOP, n_blk)):
            best = jnp.max(score, axis=0, keepdims=True)
            first = jnp.min(jnp.where(score == best, blk_f, float(n_blk)), axis=0, keepdims=True)
            hit = blk_f == first
            sel = jnp.where(hit, 1.0, sel)
            score = jnp.where(hit, -jnp.inf, score)
        sel_scr[...] = jnp.concatenate([sel] * HPG, axis=1)

        def sel_rows(first_blk, n):
            rows = [jnp.broadcast_to(sel_scr[pl.ds(first_blk + j, 1), :], (SEL_BLOCK, NL)) for j in range(n)]
            return jnp.concatenate(rows, axis=0) > 0.5

        def sel_step(keys, vals, s_tile, mask, carry):
            m_run, l_run, acc = carry
            s_tile = jnp.where(mask, s_tile, NEG)
            m_new = jnp.maximum(m_run, jnp.max(s_tile, axis=0, keepdims=True))
            alpha = jnp.exp(m_run - m_new)
            p_t = jnp.where(mask, jnp.exp(s_tile - m_new), 0.0)
            l_new = alpha * l_run + jnp.sum(p_t, axis=0, keepdims=True)
            pv = _dot_tn(vals, _bf(p_t))[g * dh:(g + 1) * dh]
            return m_new, l_new, alpha * acc + pv

        def far_body(kb, carry):
            r = pl.multiple_of(kb * KEY_TILE + WINDOW, KEY_TILE)
            keys = ks_ref[pl.ds(r, KEY_TILE), :]
            vals = vs_ref[pl.ds(r, KEY_TILE), :]
            s_tile = _dot_nt(keys, qg) + bfar
            mask = sel_rows(kb * (KEY_TILE // SEL_BLOCK), KEY_TILE // SEL_BLOCK)
            return sel_step(keys, vals, s_tile, mask, carry)

        init = (jnp.full((1, NL), NEG, _F32), jnp.zeros((1, NL), _F32), jnp.zeros((dh, NL), _F32))
        carry = lax.fori_loop(0, jnp.maximum(qb - 1, 0), far_body, init)

        rn = pl.multiple_of(c + WINDOW - KEY_TILE, KEY_TILE)
        keys = ks_ref[pl.ds(rn, 2 * KEY_TILE), :]
        vals = vs_ref[pl.ds(rn, 2 * KEY_TILE), :]
        bias_near = bs_ref[g]
        s_tile = _dot_nt(keys, qg) + bias_near
        kpos = c - KEY_TILE + lax.broadcasted_iota(jnp.int32, (2 * KEY_TILE, 1), 0)
        first_blk = jnp.maximum(qb - 1, 0) * (KEY_TILE // SEL_BLOCK)
        near_sel = jnp.concatenate([
            jnp.where(qb > 0, 1.0, 0.0) * sel_rows(first_blk, KEY_TILE // SEL_BLOCK).astype(_F32),
            sel_rows(qb * (KEY_TILE // SEL_BLOCK), KEY_TILE // SEL_BLOCK).astype(_F32)], axis=0) > 0.5
        mask = near_sel & (bias_near > MASKED_BELOW) & (kpos >= 0)
        _, l_fin, acc = sel_step(keys, vals, s_tile, mask, carry)
        o_sel = acc / jnp.where(l_fin > 0.0, l_fin, 1.0)

        rw = pl.multiple_of(c, KEY_TILE)
        n_w = WINDOW + QT
        bias_w = bw_ref[g]
        s_w = _dot_nt(kw_ref[pl.ds(rw, n_w), :], qg) + bias_w
        kpos_w = c - WINDOW + lax.broadcasted_iota(jnp.int32, (n_w, 1), 0)
        mask_w = (bias_w > MASKED_BELOW) & (kpos_w >= 0)
        s_w = jnp.where(mask_w, s_w, NEG)
        e_w = jnp.exp(s_w - jnp.max(s_w, axis=0, keepdims=True))
        p_w = jnp.where(mask_w, e_w / jnp.sum(e_w, axis=0, keepdims=True), 0.0)
        o_win = _dot_tn(vw_ref[pl.ds(rw, n_w), :], _bf(p_w))[g * dh:(g + 1) * dh]

        o_t = gate_row(0, g) * o_cmp + gate_row(1, g) * o_sel + gate_row(2, g) * o_win
        for pair in range(HPG // 2):
            blk_t = jnp.concatenate([o_t[:, (2 * pair) * QT:(2 * pair + 1) * QT],
                                     o_t[:, (2 * pair + 1) * QT:(2 * pair + 2) * QT]], axis=0)
            col = (g * HPG + 2 * pair) * dh
            oa_ref[:, col:col + LANES] = _bf(jnp.transpose(blk_t))


def _nsa_prompt(qa, gn, kcp, vcp, ks, vs, kw, vw, bc, bs, bw, bfar, ct):
    T = qa.shape[0]
    QT = QUERY_TILE
    NL = HEADS_PER_GROUP * QT
    full = lambda a: _const_spec(a.shape)
    return pl.pallas_call(
        _nsa_prompt_kernel,
        grid=(T // QT,),
        in_specs=[pl.BlockSpec((QT, NSA_WIDTH), lambda i: (i, 0)),
                  pl.BlockSpec((QT, LANES), lambda i: (i, 0)),
                  full(kcp), full(vcp), full(ks), full(vs), full(kw), full(vw),
                  full(bc), full(bs), full(bw), full(bfar), full(ct)],
        out_specs=pl.BlockSpec((QT, NSA_WIDTH), lambda i: (i, 0)),
        out_shape=jax.ShapeDtypeStruct((T, NSA_WIDTH), _BF16),
        scratch_shapes=[pltpu.VMEM((kcp.shape[0], NL), _F32), pltpu.VMEM((ct.shape[0], NL), _F32)],
        compiler_params=_params(("arbitrary",)),
        name="nsa_prompt",
    )(qa, gn, kcp, vcp, ks, vs, kw, vw, bc, bs, bw, bfar, ct)


def _sb_block(z, carry, tt, mask=None):
    sp = _softplus(z)
    if mask is not None:
        sp = jnp.where(mask, sp, 0.0)
    hi, lo = _split_hi_lo(sp)
    cs = _dot(jnp.concatenate([hi, lo], axis=1), tt)
    a = jnp.exp(z - cs[:, :LANES] - carry)
    if mask is not None:
        a = jnp.where(mask, a, 0.0)
    return a, cs[:, LANES:]


def _sb_prompt_kernel(q_ref, k_ref, v_ref, tt_ref, o_ref, acc_scr, car_scr):
    dh = HEAD_DIM
    QT = QUERY_TILE
    i = pl.program_id(1)
    lane_half = lax.broadcasted_iota(jnp.int32, (1, LANES), 1) // dh
    q = q_ref[...].astype(_F32)
    qh = [_bf(jnp.where(lane_half == h, q, 0.0)) for h in range(2)]
    tt = tt_ref[...]

    def block(r, mask):
        keys = k_ref[pl.ds(r, KEY_TILE), :]
        vals = v_ref[pl.ds(r, KEY_TILE), :]
        for h in range(2):
            a, tot = _sb_block(_dot_nt(qh[h], keys), car_scr[h], tt, mask)
            acc_scr[h] = acc_scr[h] + _dot(_bf(a), vals)
            car_scr[h] = car_scr[h] + tot

    acc_scr[...] = jnp.zeros(acc_scr.shape, _F32)
    car_scr[...] = jnp.zeros(car_scr.shape, _F32)
    qi = lax.broadcasted_iota(jnp.int32, (QT, KEY_TILE), 0)
    kj = lax.broadcasted_iota(jnp.int32, (QT, KEY_TILE), 1)
    block(pl.multiple_of(i * QT, QT), kj < qi)

    def body(n, _):
        block(pl.multiple_of((i - 1 - n) * KEY_TILE, KEY_TILE), None)
        return 0

    lax.fori_loop(0, i, body, 0)
    o_ref[...] = _bf(jnp.where(lane_half == 0, acc_scr[0], acc_scr[1]))


def _sb_prompt(qb, sbb, tt):
    T = qb.shape[0]
    QT = QUERY_TILE
    n_pair = SB_WIDTH // LANES
    return pl.pallas_call(
        _sb_prompt_kernel,
        grid=(n_pair, T // QT),
        in_specs=[pl.BlockSpec((QT, LANES), lambda p, i: (i, p)),
                  pl.BlockSpec((T, LANES), lambda p, i: (0, p)),
                  pl.BlockSpec((T, LANES), lambda p, i: (0, n_pair + p)),
                  _const_spec(tt.shape)],
        out_specs=pl.BlockSpec((QT, LANES), lambda p, i: (i, p)),
        out_shape=jax.ShapeDtypeStruct((T, SB_WIDTH), _BF16),
        scratch_shapes=[pltpu.VMEM((2, QT, LANES), _F32), pltpu.VMEM((2, QT, LANES), _F32)],
        compiler_params=_params(("arbitrary", "arbitrary")),
        name="sb_prompt",
    )(qb, sbb, sbb, tt)


def _merge_kernel(x_ref, oa_ref, ob_ref, gm_ref, wa, wb, wo, g_post, g_pre, x1_o, h2_o):
    gm = gm_ref[...]
    mixed = (jax.nn.sigmoid(gm[:, :D_MODEL]) * _dot(oa_ref[...], wa[...])
             + jax.nn.sigmoid(gm[:, D_MODEL:]) * _dot(ob_ref[...], wb[...]))
    x1 = x_ref[...] + _rms(_dot(_bf(mixed), wo[...]), g_post[...])
    x1_o[...] = x1
    h2_o[...] = _bf(_rms(x1, g_pre[...]))


def _merge(x, oa, ob, gm, wa, wb, wo, g_post, g_pre):
    rows = x.shape[0]
    tm = min(ROW_TILE, rows)
    row_spec = lambda w: pl.BlockSpec((tm, w), lambda i: (i, 0))
    return pl.pallas_call(
        _merge_kernel,
        grid=(rows // tm,),
        in_specs=[row_spec(D_MODEL), row_spec(NSA_WIDTH), row_spec(SB_WIDTH), row_spec(2 * D_MODEL),
                  _const_spec(wa.shape), _const_spec(wb.shape), _const_spec(wo.shape),
                  _const_spec((1, D_MODEL)), _const_spec((1, D_MODEL))],
        out_specs=[row_spec(D_MODEL), row_spec(D_MODEL)],
        out_shape=[jax.ShapeDtypeStruct((rows, D_MODEL), _F32), jax.ShapeDtypeStruct((rows, D_MODEL), _BF16)],
        compiler_params=_params(("arbitrary",)),
        name="merge",
    )(x, oa, ob, gm, wa, wb, wo, g_post, g_pre)


def _ffn_tail(x1, conv, gate, wd, g_post):
    f = jax.nn.gelu(conv) * gate
    return x1 + _rms(_dot(_bf(f), wd), g_post)


def _ffn_prompt_kernel(h2_ref, halo_ref, x1_ref, wua, wub, cw_ref, cb_ref, wd, g_post, y_o, tail_o):
    i = pl.program_id(0)
    tm = h2_ref.shape[0]
    h2 = h2_ref[...]
    a_ext = _dot(jnp.concatenate([halo_ref[...], h2], axis=0), wua[...])
    row = lax.broadcasted_iota(jnp.int32, (SUBLANES + tm, 1), 0)
    a_ext = jnp.where((row < SUBLANES) & (i == 0), 0.0, a_ext)
    a0 = a_ext[SUBLANES:]
    a1 = pltpu.roll(a_ext, 1, 0)[SUBLANES:]
    a2 = pltpu.roll(a_ext, 2, 0)[SUBLANES:]
    cw = cw_ref[...]
    conv = cb_ref[...] + cw[0:1] * a2 + cw[1:2] * a1 + cw[2:3] * a0
    y_o[...] = _ffn_tail(x1_ref[...], conv, _dot(h2, wub[...]), wd[...], g_post[...])
    tail_o[...] = a0[tm - SUBLANES:]


def _ffn_prompt(h2, x1, wua, wub, cw, cb, wd, g_post):
    T = h2.shape[0]
    tm = min(ROW_TILE, T)
    hb = tm // SUBLANES
    return pl.pallas_call(
        _ffn_prompt_kernel,
        grid=(T // tm,),
        in_specs=[pl.BlockSpec((tm, D_MODEL), lambda i: (i, 0)),
                  pl.BlockSpec((SUBLANES, D_MODEL), lambda i: (jnp.maximum(i * hb - 1, 0), 0)),
                  pl.BlockSpec((tm, D_MODEL), lambda i: (i, 0)),
                  _const_spec(wua.shape), _const_spec(wub.shape), _const_spec(cw.shape),
                  _const_spec(cb.shape), _const_spec(wd.shape), _const_spec((1, D_MODEL))],
        out_specs=[pl.BlockSpec((tm, D_MODEL), lambda i: (i, 0)),
                   pl.BlockSpec((SUBLANES, D_FF), lambda i: (i, 0))],
        out_shape=[jax.ShapeDtypeStruct((T, D_MODEL), _F32),
                   jax.ShapeDtypeStruct((T // tm * SUBLANES, D_FF), _F32)],
        compiler_params=_params(("arbitrary",)),
        name="ffn_prompt",
    )(h2, h2, x1, wua, wub, cw, cb, wd, g_post)


def _ffn_sample_kernel(h2_ref, x1_ref, sc_ref, wua, wub, cw_ref, cb_ref, wd, g_post, y_o, tail_o):
    db = sc_ref.shape[1]
    nq = h2_ref.shape[0] // db
    h2 = h2_ref[...]
    a0 = _dot(h2, wua[...])
    a1 = jnp.concatenate([sc_ref[1], a0[:(nq - 1) * db]], axis=0)
    a2 = jnp.concatenate([sc_ref[0], sc_ref[1], a0[:(nq - 2) * db]], axis=0)
    cw = cw_ref[...]
    conv = cb_ref[...] + cw[0:1] * a2 + cw[1:2] * a1 + cw[2:3] * a0
    y_o[...] = _ffn_tail(x1_ref[...], conv, _dot(h2, wub[...]), wd[...], g_post[...])
    tail_o[...] = a0[(nq - 2) * db:]


def _ffn_sample(h2, x1, sc, wua, wub, cw, cb, wd, g_post):
    rows = h2.shape[0]
    db = sc.shape[1]
    return pl.pallas_call(
        _ffn_sample_kernel,
        out_shape=[jax.ShapeDtypeStruct((rows, D_MODEL), _F32),
                   jax.ShapeDtypeStruct((2 * db, D_FF), _F32)],
        compiler_params=_params(),
        name="ffn_sample",
    )(h2, x1, sc, wua, wub, cw, cb, wd, g_post)


def _cmp_pages_kernel(pt_ref, *refs):
    n = len(refs) - 2
    w_ref, o_ref = refs[n], refs[n + 1]
    x = jnp.concatenate([_bf(r[0]) for r in refs[:n]], axis=0)
    o_ref[...] = _dot(x, w_ref[...])


def _cmp_pages(page_flat, cache, w):
    n_used = page_flat.shape[0]
    cpp = cache.shape[1]
    pps = min(PAGES_PER_STEP, n_used)
    page_spec = lambda j: pl.BlockSpec((1, cpp, cache.shape[2]), lambda i, pt: (pt[i * pps + j], 0, 0))
    return pl.pallas_call(
        _cmp_pages_kernel,
        grid_spec=pltpu.PrefetchScalarGridSpec(
            num_scalar_prefetch=1,
            grid=(n_used // pps,),
            in_specs=[page_spec(j) for j in range(pps)] + [pl.BlockSpec(w.shape, lambda i, pt: (0, 0))],
            out_specs=pl.BlockSpec((pps * cpp, w.shape[1]), lambda i, pt: (i, 0)),
        ),
        out_shape=jax.ShapeDtypeStruct((n_used * cpp, w.shape[1]), _F32),
        compiler_params=_params(("arbitrary",)),
        name="cmp_pages",
    )(page_flat, *([cache] * pps), w)


def _page_chunk_weights(w1k, w1v):
    G, dh = NSA_KV_GROUPS, HEAD_DIM
    w = jnp.stack([w1k.reshape(2, CMP_STRIDE, dh, dh), w1v.reshape(2, CMP_STRIDE, dh, dh)])
    big = jnp.einsum('ksLde,kK,gh->LkgdKshe', w, jnp.eye(2, dtype=w.dtype), jnp.eye(G, dtype=w.dtype))
    return big.reshape(CMP_STRIDE * 2 * G * dh, 2 * 2 * G * dh)


def _softmax_rows(parts):
    m = parts[0].max(axis=1, keepdims=True)
    for s in parts[1:]:
        m = jnp.maximum(m, s.max(axis=1, keepdims=True))
    es = [jnp.exp(s - m) for s in parts]
    l = es[0].sum(axis=1, keepdims=True)
    for e in es[1:]:
        l = l + e.sum(axis=1, keepdims=True)
    return [jnp.where(s > MASKED_BELOW, e / l, 0.0) for s, e in zip(parts, es)]


def _nsa_sample_kernel(pt_ref, *refs, n_cmp, n_sel, cur_blocks):
    pps = len(refs) - 22
    pages = refs[:pps]
    (q_ref, gate_ref, parts_ref, win_ref, new_ref, b1k, b1v, w2k, w2v,
     bcmp_ref, bwin_ref, blast_ref, bnew_ref, bfar_ref, c_ref,
     o_ref, sel_scr, oc_scr, ow_scr, m_scr, l_scr, acc_scr) = refs[pps:]
    G, HPG, dh = NSA_KV_GROUPS, HEADS_PER_GROUP, HEAD_DIM
    R = q_ref.shape[1]
    GQ = R // HPG
    st = pl.program_id(1)
    n_st = pl.num_programs(1)
    q = q_ref[0]
    n_blk = c_ref.shape[1]
    nk = pps * PAGE_SIZE

    @pl.when(st == 0)
    def _():
        parts = parts_ref[0]
        kc = _bf(_compress_finish(parts[:, 0:KV_WIDTH], parts[:, KV_WIDTH:2 * KV_WIDTH], b1k[...], w2k[...]))
        vc = _bf(_compress_finish(parts[:, 2 * KV_WIDTH:3 * KV_WIDTH], parts[:, 3 * KV_WIDTH:], b1v[...], w2v[...]))
        (p,) = _softmax_rows([_dot_nt(q, kc) + bcmp_ref[...]])
        oc_scr[...] = _dot(_bf(p), vc)
        p_grp = p[0:GQ]
        for h in range(1, HPG):
            p_grp = p_grp + p[h * GQ:(h + 1) * GQ]
        p_hi, p_lo = _split_hi_lo(p_grp)
        p_slc = _dot(p_hi, c_ref[...]) + _dot(p_lo, c_ref[...])

        blk = lax.broadcasted_iota(jnp.int32, (GQ, n_blk), 1)
        blk_f = blk.astype(_F32)
        cur = jnp.zeros((GQ, 1), jnp.int32)
        qi = lax.broadcasted_iota(jnp.int32, (GQ, 1), 0) % (GQ // G)
        for j, cb in enumerate(cur_blocks):
            cur = jnp.where(qi == j, cb, cur)
        forced = (blk == 0) | (blk == cur) | (blk == cur - 1)
        score = jnp.where(forced, FORCE_SCORE, jnp.where(blk <= cur, p_slc, NEG))
        score = jnp.where(blk < n_sel, score, -jnp.inf)
        sel = jnp.zeros((GQ, n_blk), _F32)
        for _ in range(min(SEL_TOP, n_sel)):
            best = jnp.max(score, axis=1, keepdims=True)
            first = jnp.min(jnp.where(score == best, blk_f, float(n_blk)), axis=1, keepdims=True)
            hit = blk_f == first
            sel = jnp.where(hit, 1.0, sel)
            score = jnp.where(hit, -jnp.inf, score)
        sel_scr[...] = _bf(jnp.concatenate([sel] * HPG, axis=0))

        new = new_ref[0]
        pad = jnp.zeros((PAGE_SIZE - new.shape[0], KV_WIDTH), _F32)
        col = lambda j: _bf(jnp.concatenate([new[:, j * KV_WIDTH:(j + 1) * KV_WIDTH], pad], axis=0))
        ks_new, vs_new, kw_new, vw_new = col(2), col(3), col(4), col(5)

        win = win_ref[0]
        p_w, p_wn = _softmax_rows([_dot_nt(q, _bf(win[:, :KV_WIDTH])) + bwin_ref[...],
                                   _dot_nt(q, kw_new) + bnew_ref[...]])
        ow_scr[...] = _dot(_bf(p_w), _bf(win[:, KV_WIDTH:])) + _dot(_bf(p_wn), vw_new)

        s_new = _dot_nt(q, ks_new) + bnew_ref[...]
        m0 = s_new.max(axis=1, keepdims=True)
        p_new = jnp.where(s_new > MASKED_BELOW, jnp.exp(s_new - m0), 0.0)
        m_scr[...] = m0
        l_scr[...] = p_new.sum(axis=1, keepdims=True)
        acc_scr[...] = _dot(_bf(p_new), vs_new)

    keys = jnp.concatenate([_bf(r[0][:, :KV_WIDTH]) for r in pages], axis=0)
    vals = jnp.concatenate([_bf(r[0][:, KV_WIDTH:]) for r in pages], axis=0)
    bfar = bfar_ref[...]
    is_last = st == n_st - 1
    bias = jnp.concatenate([jnp.broadcast_to(bfar, (R, nk - PAGE_SIZE)),
                            jnp.where(is_last, blast_ref[...], jnp.broadcast_to(bfar, (R, PAGE_SIZE)))], axis=1)
    s = _dot_nt(q, keys) + bias
    blk_of_key = st * (nk // SEL_BLOCK) + lax.broadcasted_iota(jnp.int32, (n_blk, nk), 1) // SEL_BLOCK
    expand = _bf(jnp.where(lax.broadcasted_iota(jnp.int32, (n_blk, nk), 0) == blk_of_key, 1.0, 0.0))
    mask = _dot(sel_scr[...], expand) > 0.5
    s = jnp.where(mask, s, NEG)
    m_new = jnp.maximum(m_scr[...], s.max(axis=1, keepdims=True))
    alpha = jnp.exp(m_scr[...] - m_new)
    p = jnp.where(mask, jnp.exp(s - m_new), 0.0)
    l_scr[...] = alpha * l_scr[...] + p.sum(axis=1, keepdims=True)
    acc_scr[...] = alpha * acc_scr[...] + _dot(_bf(p), vals)
    m_scr[...] = m_new

    @pl.when(is_last)
    def _():
        gt = jax.nn.sigmoid(gate_ref[0])
        l_fin = l_scr[...]
        o_sel = acc_scr[...] / jnp.where(l_fin > 0.0, l_fin, 1.0)
        o_ref[0] = gt[:, 0:1] * oc_scr[...] + gt[:, 1:2] * o_sel + gt[:, 2:3] * ow_scr[...]


def _nsa_sample(page_flat, cache, q32, gate32, parts, win, new8, b1k, b1v, w2k, w2v,
                bcmp, bwin, blast, bnew, bfar, cmat, n_cmp, n_sel, cur_blocks):
    db, R, _ = q32.shape
    n_pages = page_flat.shape[0] // db
    pps = min(PAGES_PER_STEP, n_pages)
    n_st = n_pages // pps
    page_spec = lambda j: pl.BlockSpec((1, PAGE_SIZE, 2 * KV_WIDTH),
                                       lambda b, s, pt: (pt[b * n_pages + s * pps + j], 0, 0))
    per_seq = lambda a: pl.BlockSpec((1,) + a.shape[1:], lambda b, s, pt: (b,) + (0,) * (a.ndim - 1))
    const = lambda a: pl.BlockSpec(a.shape, lambda b, s, pt: (0,) * a.ndim)
    kern = functools.partial(_nsa_sample_kernel, n_cmp=n_cmp, n_sel=n_sel, cur_blocks=cur_blocks)
    return pl.pallas_call(
        kern,
        grid_spec=pltpu.PrefetchScalarGridSpec(
            num_scalar_prefetch=1,
            grid=(db, n_st),
            in_specs=[page_spec(j) for j in range(pps)]
            + [per_seq(q32), per_seq(gate32), per_seq(parts), per_seq(win), per_seq(new8),
               const(b1k), const(b1v), const(w2k), const(w2v),
               const(bcmp), const(bwin), const(blast), const(bnew), const(bfar), const(cmat)],
            out_specs=pl.BlockSpec((1, R, KV_WIDTH), lambda b, s, pt: (b, 0, 0)),
            scratch_shapes=[pltpu.VMEM((R, cmat.shape[1]), _BF16),
                            pltpu.VMEM((R, KV_WIDTH), _F32), pltpu.VMEM((R, KV_WIDTH), _F32),
                            pltpu.VMEM((R, 1), _F32), pltpu.VMEM((R, 1), _F32), pltpu.VMEM((R, KV_WIDTH), _F32)],
        ),
        out_shape=jax.ShapeDtypeStruct((db, R, KV_WIDTH), _F32),
        compiler_params=_params(("arbitrary", "arbitrary")),
        name="nsa_sample",
    )(page_flat, *([cache] * pps), q32, gate32, parts, win, new8, b1k, b1v, w2k, w2v,
      bcmp, bwin, blast, bnew, bfar, cmat)


def _sb_sample_kernel(pt_ref, *refs):
    pps = len(refs) - 7
    pages = refs[:pps]
    q_ref, new_ref, nmask_ref, tt_ref, o_ref, acc_scr, car_scr = refs[pps:]
    st = pl.program_id(1)
    q = q_ref[0]
    tt = tt_ref[...]

    def block(keys, vals, mask):
        a, tot = _sb_block(_dot_nt(q, keys), car_scr[...], tt, mask)
        acc_scr[...] = acc_scr[...] + _dot(_bf(a), vals)
        car_scr[...] = car_scr[...] + tot

    @pl.when(st == 0)
    def _():
        acc_scr[...] = jnp.zeros(acc_scr.shape, _F32)
        car_scr[...] = jnp.zeros(car_scr.shape, _F32)
        new = new_ref[0]
        pad = jnp.zeros((PAGE_SIZE - new.shape[0], SB_WIDTH), _F32)
        block(_bf(jnp.concatenate([new[:, :SB_WIDTH], pad], axis=0)),
              _bf(jnp.concatenate([new[:, SB_WIDTH:], pad], axis=0)), nmask_ref[...] > 0.5)

    for r in pages:
        page = r[0]
        block(_bf(page[:, :SB_WIDTH]), _bf(page[:, SB_WIDTH:]), None)

    @pl.when(st == pl.num_programs(1) - 1)
    def _():
        o_ref[0] = acc_scr[...]


def _sb_sample(page_flat, cache, q32, new8, nmask, tt):
    db, R, _ = q32.shape
    n_pages = page_flat.shape[0] // db
    pps = min(PAGES_PER_STEP, n_pages)
    n_st = n_pages // pps
    page_spec = lambda j: pl.BlockSpec(
        (1, PAGE_SIZE, 2 * SB_WIDTH), lambda b, s, pt: (pt[b * n_pages + n_pages - 1 - (s * pps + j)], 0, 0))
    per_seq = lambda a: pl.BlockSpec((1,) + a.shape[1:], lambda b, s, pt: (b,) + (0,) * (a.ndim - 1))
    const = lambda a: pl.BlockSpec(a.shape, lambda b, s, pt: (0,) * a.ndim)
    return pl.pallas_call(
        _sb_sample_kernel,
        grid_spec=pltpu.PrefetchScalarGridSpec(
            num_scalar_prefetch=1,
            grid=(db, n_st),
            in_specs=[page_spec(j) for j in range(pps)] + [per_seq(q32), per_seq(new8), const(nmask), const(tt)],
            out_specs=pl.BlockSpec((1, R, SB_WIDTH), lambda b, s, pt: (b, 0, 0)),
            scratch_shapes=[pltpu.VMEM((R, SB_WIDTH), _F32), pltpu.VMEM((R, LANES), _F32)],
        ),
        out_shape=jax.ShapeDtypeStruct((db, R, SB_WIDTH), _F32),
        compiler_params=_params(("arbitrary", "arbitrary")),
        name="sb_sample",
    )(page_flat, *([cache] * pps), q32, new8, nmask, tt)


def _split_w_in(w_in):
    sizes = [NSA_WIDTH, 6 * KV_WIDTH, 3 * NSA_HEADS, SB_WIDTH, 2 * SB_WIDTH, 2 * D_MODEL]
    cuts = np.cumsum(sizes)[:-1].tolist()
    wqa, wkv, wgn, wqb, wsb, wgm = jnp.split(w_in, cuts, axis=1)
    wgn = jnp.pad(wgn, ((0, 0), (0, LANES - wgn.shape[1])))
    return [_bf(w) for w in (wqa, wkv, wgn, wqb, wsb, wgm)]


def _prompt_tables(rel_bias):
    G, HPG, QT = NSA_KV_GROUPS, HEADS_PER_GROUP, QUERY_TILE
    heads = (np.arange(G)[:, None, None, None] * HPG + np.arange(HPG)[None, None, :, None])
    i = np.arange(QT)[None, None, None, :]

    def table(n_rows, dist_fn, valid_fn):
        r = np.arange(n_rows)[None, :, None, None]
        dist = np.broadcast_to(dist_fn(r, i), (G, n_rows, HPG, QT))
        tab = _bias_table(rel_bias, dist, valid_fn(dist), np.broadcast_to(heads, dist.shape))
        return tab.reshape(G, n_rows, HPG * QT)

    bc = table(CMP_WIN, lambda r, i: i - CMP_STRIDE * (r - CMP_PAD) - (CMP_BLOCK - 1), lambda d: d >= 0)
    bs = table(2 * KEY_TILE, lambda r, i: i + KEY_TILE - r, lambda d: d >= 0)
    bw = table(WINDOW + QT, lambda r, i: i + WINDOW - r, lambda d: (d >= 0) & (d < WINDOW))
    far = rel_bias[N_BUCKETS - 1].reshape(G, 1, HPG, 1)
    bfar = jnp.broadcast_to(far, (G, 1, HPG, QT)).reshape(G, 1, HPG * QT).astype(_F32)
    return bc, bs, bw, bfar


def _sample_tables(rel_bias, n_chunks, n_cmp, w_buf):
    G, HPG, Q = NSA_KV_GROUPS, HEADS_PER_GROUP, DEC_SEQ
    R = HPG * G * Q
    rows = np.arange(R)
    heads = ((rows // Q) % G * HPG + rows // (G * Q))[:, None]
    t = (PAST_LEN + rows % Q)[:, None]

    def table(kpos_or_end, valid_extra):
        dist = t - kpos_or_end[None, :]
        return _bias_table(rel_bias, dist, valid_extra(dist), np.broadcast_to(heads, dist.shape))

    n = np.arange(n_chunks)
    bcmp = table(CMP_STRIDE * n + CMP_BLOCK - 1, lambda d: (d >= 0) & (n[None, :] < n_cmp))
    kw = PAST_LEN - w_buf + np.arange(w_buf)
    bwin = table(kw, lambda d: (d >= 0) & (d < WINDOW) & (kw[None, :] >= 0))
    blast = table(PAST_LEN - PAGE_SIZE + np.arange(PAGE_SIZE), lambda d: d >= 0)
    kn = np.arange(PAGE_SIZE)
    bnew = table(PAST_LEN + kn, lambda d: (d >= 0) & (kn[None, :] < Q))
    bfar = rel_bias[N_BUCKETS - 1][heads[:, 0]].reshape(R, 1).astype(_F32)
    return bcmp, bwin, blast, bnew, bfar


def kernel(x_prompt, x_sample, cache_cmp, cache_sel, cache_sb, state_win, state_conv, page_table,
           rel_bias, g_pre_mix, w_in, w_ck1, b_ck1, w_ck2, w_cv1, b_cv1, w_cv2, w_branch_a, w_branch_b,
           w_out, g_post_mix, g_pre_ffn, w_ffn_up, conv_w, conv_b, w_ffn_down, g_post_ffn):
    assert DEPTH == 1 and x_prompt.shape[0] == 1
    G, HPG, dh, Q = NSA_KV_GROUPS, HEADS_PER_GROUP, HEAD_DIM, DEC_SEQ
    T = x_prompt.shape[1]
    DB = x_sample.shape[0]
    n_pages = page_table.shape[1]
    w_buf = state_win.shape[2]
    row = lambda v: v.reshape(1, -1)

    w_proj = _split_w_in(w_in[0])
    g_pre = row(g_pre_mix[0])
    w1k, w1v = _bf(_chunk_weights(w_ck1[0])), _bf(_chunk_weights(w_cv1[0]))
    b1k, b1v = row(jnp.tile(b_ck1[0], G)), row(jnp.tile(b_cv1[0], G))
    w2k, w2v = _bf(_group_diag(w_ck2[0])), _bf(_group_diag(w_cv2[0]))
    wa, wb, wo = _bf(w_branch_a[0]), _bf(w_branch_b[0]), _bf(w_out[0])
    wua, wub = _bf(w_ffn_up[0][:, :D_FF]), _bf(w_ffn_up[0][:, D_FF:])
    wd = _bf(w_ffn_down[0])
    cw, cb = conv_w[0], row(conv_b[0])
    g_post_m, g_pre_f, g_post_f = row(g_post_mix[0]), row(g_pre_ffn[0]), row(g_post_ffn[0])
    tt = jnp.asarray(_cumsum_matrix(), _BF16)

    xp = x_prompt[0]
    qa, kv, kvb, gn, qb, sb, sbb, gm = _proj_in(xp, g_pre, w_proj)
    n_chunk = T // CMP_STRIDE
    n_cmp = (T - CMP_BLOCK) // CMP_STRIDE + 1
    kvcol = lambda a, j: a[:, j * KV_WIDTH:(j + 1) * KV_WIDTH]
    kc, vc = _compress_prompt(kvcol(kvb, 0).reshape(n_chunk, CMP_STRIDE * KV_WIDTH),
                              kvcol(kvb, 1).reshape(n_chunk, CMP_STRIDE * KV_WIDTH),
                              w1k, w1v, b1k, b1v, w2k, w2v)
    kcp = jnp.pad(kc, ((CMP_PAD, 0), (0, 0)))
    vcp = jnp.pad(vc, ((CMP_PAD, 0), (0, 0)))
    front = lambda a: jnp.pad(a, ((WINDOW, 0), (0, 0)))
    bc, bs, bw, bfar = _prompt_tables(rel_bias)
    ct = jnp.asarray(_cover_matrix(n_chunk + CMP_PAD, T // SEL_BLOCK, n_cmp, CMP_PAD).T, _BF16)
    o_a = _nsa_prompt(qa, gn, kcp, vcp, front(kvcol(kvb, 2)), front(kvcol(kvb, 3)),
                      front(kvcol(kvb, 4)), front(kvcol(kvb, 5)), bc, bs, bw, bfar, ct)
    o_b = _sb_prompt(qb, sbb, tt)
    x1, h2 = _merge(xp, o_a, o_b, gm, wa, wb, wo, g_post_m, g_pre_f)
    y_p, tails = _ffn_prompt(h2, x1, wua, wub, cw, cb, wd, g_post_f)

    kv_cmp_p = kv[:, 0:2 * KV_WIDTH].reshape(1, 1, T, 2, G, dh)
    kv_sel_p = kv[:, 2 * KV_WIDTH:4 * KV_WIDTH].reshape(1, 1, T, 2, G, dh)
    kv_sb_p = sb.reshape(1, 1, T, 2, SB_HEADS, dh)
    win_rows = jnp.pad(kv[:, 4 * KV_WIDTH:], ((WINDOW, 0), (0, 0)))[T + WINDOW - w_buf:]
    win_p = win_rows.reshape(1, 1, w_buf, 2, G, dh)
    conv_p = tails[-(CONV_W - 1):].reshape(1, 1, CONV_W - 1, D_FF)

    xs = jnp.transpose(x_sample, (1, 0, 2)).reshape(Q * DB, D_MODEL)
    qa_s, kv_s, _, gn_s, qb_s, sb_s, _, gm_s = _proj_in(xs, g_pre, w_proj)
    by_seq = lambda a: jnp.transpose(a.reshape(Q, DB, -1), (1, 0, 2))
    page_flat = page_table.reshape(-1)

    cpp = PAGE_SIZE // CMP_STRIDE
    n_chunk_s = n_pages * cpp
    n_cmp_s = (PAST_LEN + Q - CMP_BLOCK) // CMP_STRIDE + 1
    assert n_cmp_s < n_chunk_s + 1 and Q <= SUBLANES and Q <= CMP_STRIDE
    parts = _cmp_pages(page_flat, cache_cmp[0].reshape(-1, cpp, CMP_STRIDE * 2 * KV_WIDTH),
                       _bf(_page_chunk_weights(w_ck1[0], w_cv1[0])))
    parts = parts.reshape(DB, n_chunk_s, 4 * KV_WIDTH)

    q5 = by_seq(qa_s).reshape(DB, Q, G, HPG, dh)
    q5 = jnp.transpose(q5, (0, 3, 2, 1, 4))
    q32 = jnp.einsum('bhgqd,gk->bhgqkd', q5, jnp.eye(G, dtype=q5.dtype)).reshape(DB, HPG * G * Q, KV_WIDTH)
    g5 = by_seq(gn_s)[:, :, :3 * NSA_HEADS].reshape(DB, Q, 3, G, HPG)
    gate32 = jnp.transpose(g5, (0, 4, 3, 1, 2)).reshape(DB, HPG * G * Q, 3)
    gate32 = jnp.pad(gate32, ((0, 0), (0, 0), (0, LANES - 3)))
    new8 = jnp.pad(by_seq(kv_s), ((0, 0), (0, SUBLANES - Q), (0, 0)))
    n_sel_s = -(-(PAST_LEN + Q) // SEL_BLOCK)
    n_blk_s = -(-n_sel_s // LANES) * LANES
    bcmp, bwin, blast, bnew, bfar_s = _sample_tables(rel_bias, n_chunk_s, n_cmp_s, w_buf)
    cmat = jnp.asarray(_cover_matrix(n_chunk_s, n_blk_s, n_cmp_s, 0), _BF16)
    cmat = cmat * (np.arange(n_blk_s)[None, :] < n_sel_s)
    cur_blocks = tuple(int((PAST_LEN + j) // SEL_BLOCK) for j in range(Q))
    o32 = _nsa_sample(page_flat, cache_sel[0].reshape(-1, PAGE_SIZE, 2 * KV_WIDTH), q32, gate32, parts,
                      state_win[0].reshape(DB, w_buf, 2 * KV_WIDTH), new8, b1k, b1v, w2k, w2v,
                      bcmp, bwin, blast, bnew, bfar_s, cmat, n_cmp_s, n_sel_s, cur_blocks)
    o6 = o32.reshape(DB, HPG, G, Q, G, dh)
    o_a_s = jnp.stack([o6[:, :, g, :, g, :] for g in range(G)], axis=1)
    o_a_s = jnp.transpose(o_a_s, (3, 0, 1, 2, 4)).reshape(Q * DB, NSA_WIDTH)

    qb5 = jnp.transpose(by_seq(qb_s).reshape(DB, Q, SB_HEADS, dh), (0, 2, 1, 3))
    qsb = jnp.einsum('bhqd,hk->bhqkd', qb5, jnp.eye(SB_HEADS, dtype=qb5.dtype)).reshape(DB, SB_HEADS * Q, SB_WIDTH)
    sb_new8 = jnp.pad(by_seq(sb_s), ((0, 0), (0, SUBLANES - Q), (0, 0)))
    rq = np.arange(SB_HEADS * Q)[:, None] % Q
    nmask = jnp.asarray((np.arange(PAGE_SIZE)[None, :] < rq).astype(np.float32))
    osb = _sb_sample(page_flat, cache_sb[0].reshape(-1, PAGE_SIZE, 2 * SB_WIDTH), qsb, sb_new8, nmask, tt)
    o7 = osb.reshape(DB, SB_HEADS, Q, SB_HEADS, dh)
    o_b_s = jnp.stack([o7[:, h, :, h, :] for h in range(SB_HEADS)], axis=1)
    o_b_s = jnp.transpose(o_b_s, (2, 0, 1, 3)).reshape(Q * DB, SB_WIDTH)

    x1_s, h2_s = _merge(xs, _bf(o_a_s), _bf(o_b_s), gm_s, wa, wb, wo, g_post_m, g_pre_f)
    sc = jnp.transpose(state_conv[0], (1, 0, 2))
    y_s_rows, tail_s = _ffn_sample(h2_s, x1_s, sc, wua, wub, cw, cb, wd, g_post_f)

    y_s = by_seq(y_s_rows)
    kv_seq = by_seq(kv_s)
    kv_cmp_s = kv_seq[:, :, 0:2 * KV_WIDTH].reshape(1, DB, Q, 2, G, dh)
    kv_sel_s = kv_seq[:, :, 2 * KV_WIDTH:4 * KV_WIDTH].reshape(1, DB, Q, 2, G, dh)
    kv_sb_s = by_seq(sb_s).reshape(1, DB, Q, 2, SB_HEADS, dh)
    win_new = kv_seq[:, :, 4 * KV_WIDTH:].reshape(DB, Q, 2, G, dh)
    win_s = jnp.concatenate([state_win[0], win_new], axis=1)[:, -w_buf:][None]
    conv_s = jnp.transpose(tail_s.reshape(CONV_W - 1, DB, D_FF), (1, 0, 2))[None]

    return (y_p[None], y_s, kv_cmp_p, kv_sel_p, kv_sb_p, win_p, conv_p,
            kv_cmp_s, kv_sel_s, kv_sb_s, win_s, conv_s)
```

```python
import functools
import math

import numpy as np
import jax
import jax.numpy as jnp
from jax import lax
from jax.experimental import pallas as pl
from jax.experimental.pallas import tpu as pltpu

D_MODEL = 1024
SEQ = 16384
DEPTH = 1
DEC_BATCH = 128
DEC_SEQ = 4
PAST_LEN = 8192
PAGE_SIZE = 128
HEAD_DIM = 64
NSA_HEADS = 8
NSA_KV_GROUPS = 2
HEADS_PER_GROUP = NSA_HEADS // NSA_KV_GROUPS
SB_HEADS = 8
NSA_WIDTH = NSA_HEADS * HEAD_DIM
SB_WIDTH = SB_HEADS * HEAD_DIM
KV_WIDTH = NSA_KV_GROUPS * HEAD_DIM
CMP_BLOCK = 32
CMP_STRIDE = 16
SEL_BLOCK = 64
SEL_TOP = 16
SEL_COVER_W = (0.5, 1.0, 1.0, 1.0, 0.5)
WINDOW = 512
N_BUCKETS = 32
MAX_DISTANCE = 128
D_FF = 2816
CONV_W = 3
SCALE = HEAD_DIM ** -0.5
LOG2E = math.log2(math.e)
EPS = 1e-6
NEG = -1e30
FORCE_SCORE = 1e9
MASKED_BELOW = -5e29
LANES = 128
SUBLANES = 8
QUERY_TILE = 128
KEY_TILE = 128
ROW_TILE = 256
CMP_PAD = 16
CMP_WIN = 24
PAGES_PER_STEP = 16
SB_QUERY_TILE = 256
SB_KEY_CHUNK = 1024
SEL_KEY_CHUNK = 512
VMEM_LIMIT = 56 * 1024 * 1024

_F32 = jnp.float32
_BF16 = jnp.bfloat16


def _bf(x):
    return x.astype(_BF16)


def _dot(a, b):
    return jnp.dot(a, b, preferred_element_type=_F32)


def _dot_nt(a, b):
    return lax.dot_general(a, b, (((1,), (1,)), ((), ())), preferred_element_type=_F32)


def _dot_tn(a, b):
    return lax.dot_general(a, b, (((0,), (0,)), ((), ())), preferred_element_type=_F32)


def _split_hi_lo(x):
    hi = _bf(x)
    lo = _bf(x - hi.astype(_F32))
    return hi, lo


def _rms(x, g):
    return x * lax.rsqrt(jnp.mean(x * x, axis=-1, keepdims=True) + EPS) * g


def _softplus_log2(z2):
    return jnp.maximum(z2, 0.0) + jnp.log2(1.0 + jnp.exp2(-jnp.abs(z2)))


def _const_spec(shape):
    nd = len(shape)
    return pl.BlockSpec(shape, lambda *_: (0,) * nd)


def _params(sem=None):
    return pltpu.CompilerParams(dimension_semantics=sem, vmem_limit_bytes=VMEM_LIMIT)


def _bucket_np(dist):
    n = np.maximum(dist, 0)
    max_exact = N_BUCKETS // 2
    nf = np.maximum(n, 1).astype(np.float32)
    large = max_exact + (np.log(nf / max_exact) / math.log(MAX_DISTANCE / max_exact)
                         * (N_BUCKETS - max_exact)).astype(np.int32)
    return np.where(n < max_exact, n, np.minimum(large, N_BUCKETS - 1)).astype(np.int32)


def _cover_matrix(n_rows, n_cols, n_cmp, row_offset):
    r = np.arange(n_rows)[:, None] - row_offset
    j = np.arange(n_cols)[None, :]
    k = r - 4 * j + 1
    w = np.asarray(SEL_COVER_W, np.float32)
    ok = (k >= 0) & (k <= 4) & (r >= 0) & (r < n_cmp)
    return np.where(ok, w[np.clip(k, 0, 4)], 0.0).astype(np.float32)


def _cumsum_matrix():
    k = np.arange(LANES)
    tri = (k[:, None] >= k[None, :]).astype(np.float32)
    half = np.concatenate([tri, np.ones((LANES, LANES), np.float32)], axis=1)
    return np.concatenate([half, half], axis=0)


def _proj_in_kernel(x_ref, g_ref, wqa, wkv, wgn, wqb, wsb, wgm,
                    qa_o, kv_o, kvb_o, gn_o, qb_o, sb_o, sbb_o, gm_o):
    h = _bf(_rms(x_ref[...], g_ref[...]))
    qa_o[...] = _bf(_dot(h, wqa[...]) * (SCALE * LOG2E))
    kv = _dot(h, wkv[...])
    kv_o[...] = kv
    kvb_o[...] = _bf(kv)
    gn_o[...] = _dot(h, wgn[...])
    qb_o[...] = _bf(_dot(h, wqb[...]) * (SCALE * LOG2E))
    sb = _dot(h, wsb[...])
    sb_o[...] = sb
    sbb_o[...] = _bf(sb)
    gm_o[...] = _dot(h, wgm[...])


def _proj_in(x, g, ws):
    rows = x.shape[0]
    tm = min(ROW_TILE, rows)
    widths = [w.shape[1] for w in ws]
    out_shape = [
        jax.ShapeDtypeStruct((rows, widths[0]), _BF16),
        jax.ShapeDtypeStruct((rows, widths[1]), _F32),
        jax.ShapeDtypeStruct((rows, widths[1]), _BF16),
        jax.ShapeDtypeStruct((rows, widths[2]), _F32),
        jax.ShapeDtypeStruct((rows, widths[3]), _BF16),
        jax.ShapeDtypeStruct((rows, widths[4]), _F32),
        jax.ShapeDtypeStruct((rows, widths[4]), _BF16),
        jax.ShapeDtypeStruct((rows, widths[5]), _F32),
    ]
    row_spec = lambda w: pl.BlockSpec((tm, w), lambda i: (i, 0))
    return pl.pallas_call(
        _proj_in_kernel,
        grid=(rows // tm,),
        in_specs=[row_spec(D_MODEL), _const_spec((1, D_MODEL))] + [_const_spec(w.shape) for w in ws],
        out_specs=[row_spec(s.shape[1]) for s in out_shape],
        out_shape=out_shape,
        compiler_params=_params(("arbitrary",)),
        name="proj_in",
    )(x, g, *ws)


def _compress_finish(parts_lo, parts_hi, b1, w2):
    n = parts_hi.shape[0]
    hi_next = pltpu.roll(parts_hi, n - 1, 0)
    hid = parts_lo + hi_next + b1
    return _dot(_bf(jax.nn.gelu(hid)), w2)


def _compress_kernel(ck_ref, cv_ref, w1k, w1v, b1k, b1v, w2k, w2v, kc_o, vc_o):
    for c_ref, w1, b1, w2, o in ((ck_ref, w1k, b1k, w2k, kc_o), (cv_ref, w1v, b1v, w2v, vc_o)):
        parts = _dot(c_ref[...], w1[...])
        o[...] = _bf(_compress_finish(parts[:, :KV_WIDTH], parts[:, KV_WIDTH:], b1[...], w2[...]))


def _compress_prompt(ck, cv, w1k, w1v, b1k, b1v, w2k, w2v):
    n = ck.shape[0]
    return pl.pallas_call(
        _compress_kernel,
        out_shape=[jax.ShapeDtypeStruct((n, KV_WIDTH), _BF16)] * 2,
        compiler_params=_params(),
        name="compress_prompt",
    )(ck, cv, w1k, w1v, b1k, b1v, w2k, w2v)


def _chunk_weights(w1):
    G, dh = NSA_KV_GROUPS, HEAD_DIM
    w1r = w1.reshape(2, CMP_STRIDE, dh, dh)
    eye = jnp.eye(G, dtype=w1.dtype)
    big = jnp.einsum('slde,gh->lgdshe', w1r, eye)
    return big.reshape(CMP_STRIDE * G * dh, 2 * G * dh)


def _group_diag(w2):
    return jnp.kron(jnp.eye(NSA_KV_GROUPS, dtype=w2.dtype), w2)


def _nsa_prompt_kernel(qa_ref, gn_ref, kcp_ref, vcp_ref, ks_ref, vs_ref, kw_ref, vw_ref,
                       bc_ref, bs_ref, bw_ref, bfar_ref, ct_ref, oa_ref, s_scr, sel_scr, q_scr):
    G, HPG, dh = NSA_KV_GROUPS, HEADS_PER_GROUP, HEAD_DIM
    QT = QUERY_TILE
    NL = HPG * QT
    qb = pl.program_id(0)
    c = qb * QT
    n_cp = kcp_ref.shape[0]
    n_blk = ct_ref.shape[0]

    lane_half = lax.broadcasted_iota(jnp.int32, (1, LANES), 1) // dh
    gates_t = jnp.transpose(jax.nn.sigmoid(gn_ref[...]))

    def gate_row(branch, g):
        base = branch * NSA_HEADS + g * HPG
        return jnp.concatenate([gates_t[base + h:base + h + 1, :] for h in range(HPG)], axis=1)

    ti = c + lax.broadcasted_iota(jnp.int32, (1, QT), 1)
    cur = ti // SEL_BLOCK

    for g in range(G):
        q_rows = []
        for h in range(HPG):
            col = (g * HPG + h) * dh
            tile = qa_ref[:, (col // LANES) * LANES:(col // LANES + 1) * LANES].astype(_F32)
            if (col % LANES) // dh != g:
                tile = pltpu.roll(tile, dh, 1)
            q_rows.append(_bf(jnp.where(lane_half == g, tile, 0.0)))
        q_scr[g] = jnp.concatenate(q_rows, axis=0)

    o_cmp = [None] * G
    for g in range(G):
        qg = q_scr[g]
        bfar = bfar_ref[g]

        r0 = pl.multiple_of(qb * (QT // CMP_STRIDE), SUBLANES)
        s_scr[...] = _dot_nt(kcp_ref[...], qg) + bfar
        s_scr[pl.ds(r0, CMP_WIN), :] = s_scr[pl.ds(r0, CMP_WIN), :] - bfar + bc_ref[g]
        row = lax.broadcasted_iota(jnp.int32, (n_cp, 1), 0)
        s = jnp.where((row >= CMP_PAD) & (row < r0 + CMP_WIN), s_scr[...], NEG)
        m = jnp.max(s, axis=0, keepdims=True)
        e = jnp.exp2(s - m)
        p = jnp.where(s > MASKED_BELOW, e / jnp.sum(e, axis=0, keepdims=True), 0.0)
        o_cmp[g] = _dot_tn(vcp_ref[...], _bf(p))[g * dh:(g + 1) * dh]
        p_grp = p[:, 0:QT]
        for h in range(1, HPG):
            p_grp = p_grp + p[:, h * QT:(h + 1) * QT]
        p_hi, p_lo = _split_hi_lo(p_grp)
        p_slc = _dot(ct_ref[...], p_hi) + _dot(ct_ref[...], p_lo)

        blk = lax.broadcasted_iota(jnp.int32, (n_blk, 1), 0)
        blk_f = blk.astype(_F32)
        forced = (blk == 0) | (blk == cur) | (blk == cur - 1)
        score = jnp.where(forced, FORCE_SCORE, jnp.where(blk <= cur, p_slc, NEG))
        sel = jnp.zeros((n_blk, QT), _F32)
        for _ in range(min(SEL_TOP, n_blk)):
            best = jnp.max(score, axis=0, keepdims=True)
            first = jnp.min(jnp.where(score == best, blk_f, float(n_blk)), axis=0, keepdims=True)
            hit = blk_f == first
            sel = jnp.where(hit, 1.0, sel)
            score = jnp.where(hit, -jnp.inf, score)
        sel_scr[g] = jnp.concatenate([sel] * HPG, axis=1)

    def sel_rows(g, first_blk, n):
        rows = [jnp.broadcast_to(sel_scr[g, pl.ds(first_blk + j, 1), :], (SEL_BLOCK, NL)) for j in range(n)]
        return jnp.concatenate(rows, axis=0) > 0.5

    def sel_step(g, vals, s_tile, mask, carry, zero_masked):
        m_run, l_run, acc = carry
        s_tile = jnp.where(mask, s_tile, NEG)
        m_new = jnp.maximum(m_run, jnp.max(s_tile, axis=0, keepdims=True))
        alpha = jnp.exp2(m_run - m_new)
        p_t = jnp.exp2(s_tile - m_new)
        if zero_masked:
            p_t = jnp.where(mask, p_t, 0.0)
        l_new = alpha * l_run + jnp.sum(p_t, axis=0, keepdims=True)
        pv = _dot_tn(vals, _bf(p_t))[g * dh:(g + 1) * dh]
        return m_new, l_new, alpha * acc + pv

    n_far_blk = jnp.maximum(qb - 1, 0) * (KEY_TILE // SEL_BLOCK)
    blk_per_chunk = SEL_KEY_CHUNK // SEL_BLOCK

    def far_body(f, carry):
        r = pl.multiple_of(f * SEL_KEY_CHUNK + WINDOW, KEY_TILE)
        keys = ks_ref[pl.ds(r, SEL_KEY_CHUNK), :]
        vals = vs_ref[pl.ds(r, SEL_KEY_CHUNK), :]
        out = []
        for g in range(G):
            s_tile = _dot_nt(keys, q_scr[g]) + bfar_ref[g]
            rows = []
            for j in range(blk_per_chunk):
                b = f * blk_per_chunk + j
                flag = jnp.where(b < n_far_blk, sel_scr[g, pl.ds(jnp.minimum(b, n_blk - 1), 1), :], 0.0)
                rows.append(jnp.broadcast_to(flag, (SEL_BLOCK, NL)))
            mask = jnp.concatenate(rows, axis=0) > 0.5
            out.append(sel_step(g, vals, s_tile, mask, carry[g], False))
        return tuple(out)

    init = tuple((jnp.full((1, NL), NEG, _F32), jnp.zeros((1, NL), _F32), jnp.zeros((dh, NL), _F32))
                 for _ in range(G))
    n_far_chunks = (n_far_blk + blk_per_chunk - 1) // blk_per_chunk
    far = lax.fori_loop(0, n_far_chunks, far_body, init)

    for g in range(G):
        qg = q_scr[g]
        rn = pl.multiple_of(c + WINDOW - KEY_TILE, KEY_TILE)
        keys = ks_ref[pl.ds(rn, 2 * KEY_TILE), :]
        vals = vs_ref[pl.ds(rn, 2 * KEY_TILE), :]
        bias_near = bs_ref[g]
        s_tile = _dot_nt(keys, qg) + bias_near
        kpos = c - KEY_TILE + lax.broadcasted_iota(jnp.int32, (2 * KEY_TILE, 1), 0)
        first_blk = jnp.maximum(qb - 1, 0) * (KEY_TILE // SEL_BLOCK)
        near_sel = jnp.concatenate([
            jnp.where(qb > 0, 1.0, 0.0) * sel_rows(g, first_blk, KEY_TILE // SEL_BLOCK).astype(_F32),
            sel_rows(g, qb * (KEY_TILE // SEL_BLOCK), KEY_TILE // SEL_BLOCK).astype(_F32)], axis=0) > 0.5
        mask = near_sel & (bias_near > MASKED_BELOW) & (kpos >= 0)
        _, l_fin, acc = sel_step(g, vals, s_tile, mask, far[g], True)
        o_sel = acc / jnp.where(l_fin > 0.0, l_fin, 1.0)

        rw = pl.multiple_of(c, KEY_TILE)
        n_w = WINDOW + QT
        bias_w = bw_ref[g]
        s_w = _dot_nt(kw_ref[pl.ds(rw, n_w), :], qg) + bias_w
        kpos_w = c - WINDOW + lax.broadcasted_iota(jnp.int32, (n_w, 1), 0)
        mask_w = (bias_w > MASKED_BELOW) & (kpos_w >= 0)
        s_w = jnp.where(mask_w, s_w, NEG)
        e_w = jnp.exp2(s_w - jnp.max(s_w, axis=0, keepdims=True))
        p_w = jnp.where(mask_w, e_w / jnp.sum(e_w, axis=0, keepdims=True), 0.0)
        o_win = _dot_tn(vw_ref[pl.ds(rw, n_w), :], _bf(p_w))[g * dh:(g + 1) * dh]

        o_t = gate_row(0, g) * o_cmp[g] + gate_row(1, g) * o_sel + gate_row(2, g) * o_win
        for pair in range(HPG // 2):
            blk_t = jnp.concatenate([o_t[:, (2 * pair) * QT:(2 * pair + 1) * QT],
                                     o_t[:, (2 * pair + 1) * QT:(2 * pair + 2) * QT]], axis=0)
            col = (g * HPG + 2 * pair) * dh
            oa_ref[:, col:col + LANES] = _bf(jnp.transpose(blk_t))


def _nsa_prompt(qa, gn, kcp, vcp, ks, vs, kw, vw, bc, bs, bw, bfar, ct):
    T = qa.shape[0]
    QT = QUERY_TILE
    NL = HEADS_PER_GROUP * QT
    full = lambda a: _const_spec(a.shape)
    return pl.pallas_call(
        _nsa_prompt_kernel,
        grid=(T // QT,),
        in_specs=[pl.BlockSpec((QT, NSA_WIDTH), lambda i: (i, 0)),
                  pl.BlockSpec((QT, LANES), lambda i: (i, 0)),
                  full(kcp), full(vcp), full(ks), full(vs), full(kw), full(vw),
                  full(bc), full(bs), full(bw), full(bfar), full(ct)],
        out_specs=pl.BlockSpec((QT, NSA_WIDTH), lambda i: (i, 0)),
        out_shape=jax.ShapeDtypeStruct((T, NSA_WIDTH), _BF16),
        scratch_shapes=[pltpu.VMEM((kcp.shape[0], NL), _F32),
                        pltpu.VMEM((NSA_KV_GROUPS, ct.shape[0], NL), _F32),
                        pltpu.VMEM((NSA_KV_GROUPS, NL, LANES), _BF16)],
        compiler_params=_params(("arbitrary",)),
        name="nsa_prompt",
    )(qa, gn, kcp, vcp, ks, vs, kw, vw, bc, bs, bw, bfar, ct)


def _sb_chunk(z, carry, tt, mask, latest_first):
    nb = z.shape[1] // LANES
    sp = _softplus_log2(z)
    if mask is not None:
        sp = jnp.where(mask, sp, 0.0)
    hi, lo = _split_hi_lo(sp)
    a_blocks = [None] * nb
    for j in (range(nb) if latest_first else reversed(range(nb))):
        sl = slice(j * LANES, (j + 1) * LANES)
        cs = _dot(jnp.concatenate([hi[:, sl], lo[:, sl]], axis=1), tt)
        a_blocks[j] = jnp.exp2(z[:, sl] - cs[:, :LANES] - carry)
        carry = carry + cs[:, LANES:]
    a = a_blocks[0] if nb == 1 else jnp.concatenate(a_blocks, axis=1)
    if mask is not None:
        a = jnp.where(mask, a, 0.0)
    return a, carry


def _sb_prompt_kernel(q_ref, k_ref, v_ref, tt_ref, o_ref, acc_scr, car_scr):
    dh = HEAD_DIM
    QT = q_ref.shape[0]
    KW = min(SB_KEY_CHUNK, k_ref.shape[0])
    i = pl.program_id(1)
    lane_half = lax.broadcasted_iota(jnp.int32, (1, LANES), 1) // dh
    q = q_ref[...].astype(_F32)
    qh = [_bf(jnp.where(lane_half == h, q, 0.0)) for h in range(2)]
    tt = tt_ref[...]
    last = (i * QT + QT - 1) // KW

    def chunk(ci, mask):
        r = pl.multiple_of(ci * KW, KW)
        keys = k_ref[pl.ds(r, KW), :]
        vals = v_ref[pl.ds(r, KW), :]
        for h in range(2):
            a, carry = _sb_chunk(_dot_nt(qh[h], keys), car_scr[h], tt, mask, latest_first=False)
            acc_scr[h] = acc_scr[h] + _dot(_bf(a), vals)
            car_scr[h] = carry

    acc_scr[...] = jnp.zeros(acc_scr.shape, _F32)
    car_scr[...] = jnp.zeros(car_scr.shape, _F32)
    t = i * QT + lax.broadcasted_iota(jnp.int32, (QT, 1), 0)
    kpos = last * KW + lax.broadcasted_iota(jnp.int32, (1, KW), 1)
    chunk(last, kpos < t)

    def body(n, _):
        chunk(last - 1 - n, None)
        return 0

    lax.fori_loop(0, last, body, 0)
    o_ref[...] = _bf(jnp.where(lane_half == 0, acc_scr[0], acc_scr[1]))


def _sb_prompt(qb, sbb, tt):
    T = qb.shape[0]
    QT = min(SB_QUERY_TILE, T)
    n_pair = SB_WIDTH // LANES
    return pl.pallas_call(
        _sb_prompt_kernel,
        grid=(n_pair, T // QT),
        in_specs=[pl.BlockSpec((QT, LANES), lambda p, i: (i, p)),
                  pl.BlockSpec((T, LANES), lambda p, i: (0, p)),
                  pl.BlockSpec((T, LANES), lambda p, i: (0, n_pair + p)),
                  _const_spec(tt.shape)],
        out_specs=pl.BlockSpec((QT, LANES), lambda p, i: (i, p)),
        out_shape=jax.ShapeDtypeStruct((T, SB_WIDTH), _BF16),
        scratch_shapes=[pltpu.VMEM((2, QT, LANES), _F32), pltpu.VMEM((2, QT, LANES), _F32)],
        compiler_params=_params(("arbitrary", "arbitrary")),
        name="sb_prompt",
    )(qb, sbb, sbb, tt)


def _merge_kernel(x_ref, oa_ref, ob_ref, gm_ref, wa, wb, wo, g_post, g_pre, x1_o, h2_o):
    gm = gm_ref[...]
    mixed = (jax.nn.sigmoid(gm[:, :D_MODEL]) * _dot(oa_ref[...], wa[...])
             + jax.nn.sigmoid(gm[:, D_MODEL:]) * _dot(ob_ref[...], wb[...]))
    x1 = x_ref[...] + _rms(_dot(_bf(mixed), wo[...]), g_post[...])
    x1_o[...] = x1
    h2_o[...] = _bf(_rms(x1, g_pre[...]))


def _merge(x, oa, ob, gm, wa, wb, wo, g_post, g_pre):
    rows = x.shape[0]
    tm = min(ROW_TILE, rows)
    row_spec = lambda w: pl.BlockSpec((tm, w), lambda i: (i, 0))
    return pl.pallas_call(
        _merge_kernel,
        grid=(rows // tm,),
        in_specs=[row_spec(D_MODEL), row_spec(NSA_WIDTH), row_spec(SB_WIDTH), row_spec(2 * D_MODEL),
                  _const_spec(wa.shape), _const_spec(wb.shape), _const_spec(wo.shape),
                  _const_spec((1, D_MODEL)), _const_spec((1, D_MODEL))],
        out_specs=[row_spec(D_MODEL), row_spec(D_MODEL)],
        out_shape=[jax.ShapeDtypeStruct((rows, D_MODEL), _F32), jax.ShapeDtypeStruct((rows, D_MODEL), _BF16)],
        compiler_params=_params(("arbitrary",)),
        name="merge",
    )(x, oa, ob, gm, wa, wb, wo, g_post, g_pre)


def _ffn_tail(x1, conv, gate, wd, g_post):
    f = jax.nn.gelu(conv) * gate
    return x1 + _rms(_dot(_bf(f), wd), g_post)


def _ffn_prompt_kernel(h2_ref, halo_ref, x1_ref, wua, wub, cw_ref, cb_ref, wd, g_post, y_o, tail_o):
    i = pl.program_id(0)
    tm = h2_ref.shape[0]
    h2 = h2_ref[...]
    a_ext = _dot(jnp.concatenate([halo_ref[...], h2], axis=0), wua[...])
    row = lax.broadcasted_iota(jnp.int32, (SUBLANES + tm, 1), 0)
    a_ext = jnp.where((row < SUBLANES) & (i == 0), 0.0, a_ext)
    a0 = a_ext[SUBLANES:]
    a1 = pltpu.roll(a_ext, 1, 0)[SUBLANES:]
    a2 = pltpu.roll(a_ext, 2, 0)[SUBLANES:]
    cw = cw_ref[...]
    conv = cb_ref[...] + cw[0:1] * a2 + cw[1:2] * a1 + cw[2:3] * a0
    y_o[...] = _ffn_tail(x1_ref[...], conv, _dot(h2, wub[...]), wd[...], g_post[...])
    tail_o[...] = a0[tm - SUBLANES:]


def _ffn_prompt(h2, x1, wua, wub, cw, cb, wd, g_post):
    T = h2.shape[0]
    tm = min(ROW_TILE, T)
    hb = tm // SUBLANES
    return pl.pallas_call(
        _ffn_prompt_kernel,
        grid=(T // tm,),
        in_specs=[pl.BlockSpec((tm, D_MODEL), lambda i: (i, 0)),
                  pl.BlockSpec((SUBLANES, D_MODEL), lambda i: (jnp.maximum(i * hb - 1, 0), 0)),
                  pl.BlockSpec((tm, D_MODEL), lambda i: (i, 0)),
                  _const_spec(wua.shape), _const_spec(wub.shape), _const_spec(cw.shape),
                  _const_spec(cb.shape), _const_spec(wd.shape), _const_spec((1, D_MODEL))],
        out_specs=[pl.BlockSpec((tm, D_MODEL), lambda i: (i, 0)),
                   pl.BlockSpec((SUBLANES, D_FF), lambda i: (i, 0))],
        out_shape=[jax.ShapeDtypeStruct((T, D_MODEL), _F32),
                   jax.ShapeDtypeStruct((T // tm * SUBLANES, D_FF), _F32)],
        compiler_params=_params(("arbitrary",)),
        name="ffn_prompt",
    )(h2, h2, x1, wua, wub, cw, cb, wd, g_post)


def _ffn_sample_kernel(h2_ref, x1_ref, sc_ref, wua, wub, cw_ref, cb_ref, wd, g_post, y_o, tail_o):
    db = sc_ref.shape[1]
    nq = h2_ref.shape[0] // db
    h2 = h2_ref[...]
    a0 = _dot(h2, wua[...])
    a1 = jnp.concatenate([sc_ref[1], a0[:(nq - 1) * db]], axis=0)
    a2 = jnp.concatenate([sc_ref[0], sc_ref[1], a0[:(nq - 2) * db]], axis=0)
    cw = cw_ref[...]
    conv = cb_ref[...] + cw[0:1] * a2 + cw[1:2] * a1 + cw[2:3] * a0
    y_o[...] = _ffn_tail(x1_ref[...], conv, _dot(h2, wub[...]), wd[...], g_post[...])
    tail_o[...] = a0[(nq - 2) * db:]


def _ffn_sample(h2, x1, sc, wua, wub, cw, cb, wd, g_post):
    rows = h2.shape[0]
    db = sc.shape[1]
    return pl.pallas_call(
        _ffn_sample_kernel,
        out_shape=[jax.ShapeDtypeStruct((rows, D_MODEL), _F32),
                   jax.ShapeDtypeStruct((2 * db, D_FF), _F32)],
        compiler_params=_params(),
        name="ffn_sample",
    )(h2, x1, sc, wua, wub, cw, cb, wd, g_post)


def _cmp_pages_kernel(pt_ref, *refs):
    n = len(refs) - 3
    w_ref, o_ref, x_scr = refs[n:]
    cpp = PAGE_SIZE // CMP_STRIDE
    for j, r in enumerate(refs[:n]):
        for kv in range(2):
            x_scr[kv, j * PAGE_SIZE:(j + 1) * PAGE_SIZE, :] = jnp.transpose(r[0, kv])
    for kv in range(2):
        acc = jnp.zeros((n * cpp, 2 * KV_WIDTH), _F32)
        for l in range(CMP_STRIDE):
            acc = acc + _dot(_bf(x_scr[kv, pl.ds(l, n * cpp, stride=CMP_STRIDE), :]), w_ref[l, kv])
        o_ref[:, kv * 2 * KV_WIDTH:(kv + 1) * 2 * KV_WIDTH] = acc


def _cmp_pages(page_flat, cache_t, w):
    n_used = page_flat.shape[0]
    cpp = PAGE_SIZE // CMP_STRIDE
    pps = min(PAGES_PER_STEP, n_used)
    page_spec = lambda j: pl.BlockSpec((1,) + cache_t.shape[1:], lambda i, pt: (pt[i * pps + j], 0, 0, 0))
    return pl.pallas_call(
        _cmp_pages_kernel,
        grid_spec=pltpu.PrefetchScalarGridSpec(
            num_scalar_prefetch=1,
            grid=(n_used // pps,),
            in_specs=[page_spec(j) for j in range(pps)] + [pl.BlockSpec(w.shape, lambda i, pt: (0, 0, 0, 0))],
            out_specs=pl.BlockSpec((pps * cpp, 4 * KV_WIDTH), lambda i, pt: (i, 0)),
            scratch_shapes=[pltpu.VMEM((2, pps * PAGE_SIZE, KV_WIDTH), _F32)],
        ),
        out_shape=jax.ShapeDtypeStruct((n_used * cpp, 4 * KV_WIDTH), _F32),
        compiler_params=_params(("arbitrary",)),
        name="cmp_pages",
    )(page_flat, *([cache_t] * pps), w)


def _page_chunk_weights(w1k, w1v):
    G, dh = NSA_KV_GROUPS, HEAD_DIM
    w = jnp.stack([w1k.reshape(2, CMP_STRIDE, dh, dh), w1v.reshape(2, CMP_STRIDE, dh, dh)])
    big = jnp.einsum('ksLde,gh->Lkgdshe', w, jnp.eye(G, dtype=w.dtype))
    return big.reshape(CMP_STRIDE, 2, G * dh, 2 * G * dh)


def _pages_last(cache):
    n, p, kv, h, d = cache.shape
    return jnp.transpose(cache, (0, 2, 3, 4, 1)).reshape(n, kv, h * d, p)


def _softmax_rows(parts):
    m = parts[0].max(axis=1, keepdims=True)
    for s in parts[1:]:
        m = jnp.maximum(m, s.max(axis=1, keepdims=True))
    es = [jnp.exp2(s - m) for s in parts]
    l = es[0].sum(axis=1, keepdims=True)
    for e in es[1:]:
        l = l + e.sum(axis=1, keepdims=True)
    return [jnp.where(s > MASKED_BELOW, e / l, 0.0) for s, e in zip(parts, es)]


def _nsa_sample_kernel(pt_ref, *refs, n_cmp, n_sel, cur_blocks):
    pps = len(refs) - 22
    pages = refs[:pps]
    (q_ref, gate_ref, parts_ref, win_ref, new_ref, b1k, b1v, w2k, w2v,
     bcmp_ref, bwin_ref, blast_ref, bnew_ref, bfar_ref, c_ref,
     o_ref, sel_scr, oc_scr, ow_scr, m_scr, l_scr, acc_scr) = refs[pps:]
    G, HPG, dh = NSA_KV_GROUPS, HEADS_PER_GROUP, HEAD_DIM
    R = q_ref.shape[1]
    GQ = R // HPG
    st = pl.program_id(1)
    n_st = pl.num_programs(1)
    q = q_ref[0]
    n_blk = c_ref.shape[1]
    nk = pps * PAGE_SIZE

    @pl.when(st == 0)
    def _():
        parts = parts_ref[0]
        kc = _bf(_compress_finish(parts[:, 0:KV_WIDTH], parts[:, KV_WIDTH:2 * KV_WIDTH], b1k[...], w2k[...]))
        vc = _bf(_compress_finish(parts[:, 2 * KV_WIDTH:3 * KV_WIDTH], parts[:, 3 * KV_WIDTH:], b1v[...], w2v[...]))
        (p,) = _softmax_rows([_dot_nt(q, kc) + bcmp_ref[...]])
        oc_scr[...] = _dot(_bf(p), vc)
        p_grp = p[0:GQ]
        for h in range(1, HPG):
            p_grp = p_grp + p[h * GQ:(h + 1) * GQ]
        p_hi, p_lo = _split_hi_lo(p_grp)
        p_slc = _dot(p_hi, c_ref[...]) + _dot(p_lo, c_ref[...])

        blk = lax.broadcasted_iota(jnp.int32, (GQ, n_blk), 1)
        blk_f = blk.astype(_F32)
        cur = jnp.zeros((GQ, 1), jnp.int32)
        qi = lax.broadcasted_iota(jnp.int32, (GQ, 1), 0) % (GQ // G)
        for j, cb in enumerate(cur_blocks):
            cur = jnp.where(qi == j, cb, cur)
        forced = (blk == 0) | (blk == cur) | (blk == cur - 1)
        score = jnp.where(forced, FORCE_SCORE, jnp.where(blk <= cur, p_slc, NEG))
        score = jnp.where(blk < n_sel, score, -jnp.inf)
        sel = jnp.zeros((GQ, n_blk), _F32)
        for _ in range(min(SEL_TOP, n_sel)):
            best = jnp.max(score, axis=1, keepdims=True)
            first = jnp.min(jnp.where(score == best, blk_f, float(n_blk)), axis=1, keepdims=True)
            hit = blk_f == first
            sel = jnp.where(hit, 1.0, sel)
            score = jnp.where(hit, -jnp.inf, score)
        sel_scr[...] = _bf(jnp.concatenate([sel] * HPG, axis=0))

        new = new_ref[0]
        pad = jnp.zeros((PAGE_SIZE - new.shape[0], KV_WIDTH), _F32)
        col = lambda j: _bf(jnp.concatenate([new[:, j * KV_WIDTH:(j + 1) * KV_WIDTH], pad], axis=0))
        ks_new, vs_new, kw_new, vw_new = col(2), col(3), col(4), col(5)

        p_w, p_wn = _softmax_rows([_dot(q, _bf(win_ref[0, 0])) + bwin_ref[...],
                                   _dot_nt(q, kw_new) + bnew_ref[...]])
        ow_scr[...] = _dot_nt(_bf(p_w), _bf(win_ref[0, 1])) + _dot(_bf(p_wn), vw_new)

        s_new = _dot_nt(q, ks_new) + bnew_ref[...]
        m0 = s_new.max(axis=1, keepdims=True)
        p_new = jnp.where(s_new > MASKED_BELOW, jnp.exp2(s_new - m0), 0.0)
        m_scr[...] = m0
        l_scr[...] = p_new.sum(axis=1, keepdims=True)
        acc_scr[...] = _dot(_bf(p_new), vs_new)

    keys_t = jnp.concatenate([_bf(r[0, 0]) for r in pages], axis=1)
    vals_t = jnp.concatenate([_bf(r[0, 1]) for r in pages], axis=1)
    bfar = bfar_ref[...]
    is_last = st == n_st - 1
    bias = jnp.concatenate([bfar] * (pps - 1) + [jnp.where(is_last, blast_ref[...], bfar)], axis=1)
    s = _dot(q, keys_t) + bias
    blk_of_key = st * (nk // SEL_BLOCK) + lax.broadcasted_iota(jnp.int32, (n_blk, nk), 1) // SEL_BLOCK
    expand = _bf(jnp.where(lax.broadcasted_iota(jnp.int32, (n_blk, nk), 0) == blk_of_key, 1.0, 0.0))
    mask = _dot(sel_scr[...], expand) > 0.5
    s = jnp.where(mask, s, NEG)
    m_new = jnp.maximum(m_scr[...], s.max(axis=1, keepdims=True))
    alpha = jnp.exp2(m_scr[...] - m_new)
    p = jnp.where(mask, jnp.exp2(s - m_new), 0.0)
    l_scr[...] = alpha * l_scr[...] + p.sum(axis=1, keepdims=True)
    acc_scr[...] = alpha * acc_scr[...] + _dot_nt(_bf(p), vals_t)
    m_scr[...] = m_new

    @pl.when(is_last)
    def _():
        gt = jax.nn.sigmoid(gate_ref[0])
        l_fin = l_scr[...]
        o_sel = acc_scr[...] / jnp.where(l_fin > 0.0, l_fin, 1.0)
        o_ref[0] = gt[:, 0:1] * oc_scr[...] + gt[:, 1:2] * o_sel + gt[:, 2:3] * ow_scr[...]


def _nsa_sample(page_flat, cache, q32, gate32, parts, win, new8, b1k, b1v, w2k, w2v,
                bcmp, bwin, blast, bnew, bfar, cmat, n_cmp, n_sel, cur_blocks):
    db, R, _ = q32.shape
    n_pages = page_flat.shape[0] // db
    pps = min(PAGES_PER_STEP, n_pages)
    n_st = n_pages // pps
    page_spec = lambda j: pl.BlockSpec((1,) + cache.shape[1:],
                                       lambda b, s, pt: (pt[b * n_pages + s * pps + j], 0, 0, 0))
    per_seq = lambda a: pl.BlockSpec((1,) + a.shape[1:], lambda b, s, pt: (b,) + (0,) * (a.ndim - 1))
    const = lambda a: pl.BlockSpec(a.shape, lambda b, s, pt: (0,) * a.ndim)
    kern = functools.partial(_nsa_sample_kernel, n_cmp=n_cmp, n_sel=n_sel, cur_blocks=cur_blocks)
    return pl.pallas_call(
        kern,
        grid_spec=pltpu.PrefetchScalarGridSpec(
            num_scalar_prefetch=1,
            grid=(db, n_st),
            in_specs=[page_spec(j) for j in range(pps)]
            + [per_seq(q32), per_seq(gate32), per_seq(parts), per_seq(win), per_seq(new8),
               const(b1k), const(b1v), const(w2k), const(w2v),
               const(bcmp), const(bwin), const(blast), const(bnew), const(bfar), const(cmat)],
            out_specs=pl.BlockSpec((1, R, KV_WIDTH), lambda b, s, pt: (b, 0, 0)),
            scratch_shapes=[pltpu.VMEM((R, cmat.shape[1]), _BF16),
                            pltpu.VMEM((R, KV_WIDTH), _F32), pltpu.VMEM((R, KV_WIDTH), _F32),
                            pltpu.VMEM((R, 1), _F32), pltpu.VMEM((R, 1), _F32), pltpu.VMEM((R, KV_WIDTH), _F32)],
        ),
        out_shape=jax.ShapeDtypeStruct((db, R, KV_WIDTH), _F32),
        compiler_params=_params(("arbitrary", "arbitrary")),
        name="nsa_sample",
    )(page_flat, *([cache] * pps), q32, gate32, parts, win, new8, b1k, b1v, w2k, w2v,
      bcmp, bwin, blast, bnew, bfar, cmat)


def _sb_sample_kernel(pt_ref, *refs):
    pps = len(refs) - 7
    pages = refs[:pps]
    q_ref, new_ref, nmask_ref, tt_ref, o_ref, acc_scr, car_scr = refs[pps:]
    st = pl.program_id(1)
    q = q_ref[0]
    tt = tt_ref[...]

    @pl.when(st == 0)
    def _():
        new = new_ref[0]
        pad = jnp.zeros((PAGE_SIZE - new.shape[0], SB_WIDTH), _F32)
        keys = _bf(jnp.concatenate([new[:, :SB_WIDTH], pad], axis=0))
        vals = _bf(jnp.concatenate([new[:, SB_WIDTH:], pad], axis=0))
        a, carry = _sb_chunk(_dot_nt(q, keys), jnp.zeros(car_scr.shape, _F32), tt, nmask_ref[...] > 0.5, True)
        acc_scr[...] = _dot(_bf(a), vals)
        car_scr[...] = carry

    keys_t = jnp.concatenate([_bf(r[0, 0]) for r in pages], axis=1)
    vals_t = jnp.concatenate([_bf(r[0, 1]) for r in pages], axis=1)
    a, carry = _sb_chunk(_dot(q, keys_t), car_scr[...], tt, None, True)
    acc_scr[...] = acc_scr[...] + _dot_nt(_bf(a), vals_t)
    car_scr[...] = carry

    @pl.when(st == pl.num_programs(1) - 1)
    def _():
        o_ref[0] = acc_scr[...]


def _sb_sample(page_flat, cache, q32, new8, nmask, tt):
    db, R, _ = q32.shape
    n_pages = page_flat.shape[0] // db
    pps = min(PAGES_PER_STEP, n_pages)
    n_st = n_pages // pps
    page_spec = lambda j: pl.BlockSpec(
        (1,) + cache.shape[1:], lambda b, s, pt: (pt[b * n_pages + n_pages - 1 - (s * pps + j)], 0, 0, 0))
    per_seq = lambda a: pl.BlockSpec((1,) + a.shape[1:], lambda b, s, pt: (b,) + (0,) * (a.ndim - 1))
    const = lambda a: pl.BlockSpec(a.shape, lambda b, s, pt: (0,) * a.ndim)
    return pl.pallas_call(
        _sb_sample_kernel,
        grid_spec=pltpu.PrefetchScalarGridSpec(
            num_scalar_prefetch=1,
            grid=(db, n_st),
            in_specs=[page_spec(j) for j in range(pps)] + [per_seq(q32), per_seq(new8), const(nmask), const(tt)],
            out_specs=pl.BlockSpec((1, R, SB_WIDTH), lambda b, s, pt: (b, 0, 0)),
            scratch_shapes=[pltpu.VMEM((R, SB_WIDTH), _F32), pltpu.VMEM((R, LANES), _F32)],
        ),
        out_shape=jax.ShapeDtypeStruct((db, R, SB_WIDTH), _F32),
        compiler_params=_params(("arbitrary", "arbitrary")),
        name="sb_sample",
    )(page_flat, *([cache] * pps), q32, new8, nmask, tt)


def _split_w_in(w_in):
    sizes = [NSA_WIDTH, 6 * KV_WIDTH, 3 * NSA_HEADS, SB_WIDTH, 2 * SB_WIDTH, 2 * D_MODEL]
    cuts = np.cumsum(sizes)[:-1].tolist()
    wqa, wkv, wgn, wqb, wsb, wgm = jnp.split(w_in, cuts, axis=1)
    wgn = jnp.pad(wgn, ((0, 0), (0, LANES - wgn.shape[1])))
    return [_bf(w) for w in (wqa, wkv, wgn, wqb, wsb, wgm)]


def _bucket_starts():
    b = _bucket_np(np.arange(4 * MAX_DISTANCE))
    assert b.max() == N_BUCKETS - 1 and (np.diff(b) >= 0).all()
    return [int(np.argmax(b >= k)) for k in range(1, N_BUCKETS)]


def _bias_of(dist, rb_ref, head, starts):
    out = jnp.full(dist.shape, rb_ref[N_BUCKETS - 1, head] * LOG2E, _F32)
    for b in range(N_BUCKETS - 2, -1, -1):
        out = jnp.where(dist < starts[b], rb_ref[b, head] * LOG2E, out)
    return out


def _bias_tables_kernel(rb_ref, bc_o, bs_o, bw_o, bfar_o, scmp_o, swin_o, slast_o, snew_o, sfar_o, *, n_cmp_s):
    G, HPG, QT, Q = NSA_KV_GROUPS, HEADS_PER_GROUP, QUERY_TILE, DEC_SEQ
    starts = _bucket_starts()

    qi = lax.broadcasted_iota(jnp.int32, (1, QT), 1)

    def prompt_table(o_ref, dist_fn, valid_fn):
        r = lax.broadcasted_iota(jnp.int32, (o_ref.shape[1], 1), 0)
        dist = dist_fn(r, qi)
        valid = valid_fn(dist)
        for g in range(G):
            for h in range(HPG):
                o_ref[g, :, h * QT:(h + 1) * QT] = jnp.where(valid, _bias_of(dist, rb_ref, g * HPG + h, starts), NEG)

    prompt_table(bc_o, lambda r, i: i - CMP_STRIDE * (r - CMP_PAD) - (CMP_BLOCK - 1), lambda d: d >= 0)
    prompt_table(bs_o, lambda r, i: i + KEY_TILE - r, lambda d: d >= 0)
    prompt_table(bw_o, lambda r, i: i + WINDOW - r, lambda d: (d >= 0) & (d < WINDOW))
    for g in range(G):
        for h in range(HPG):
            bfar_o[g, :, h * QT:(h + 1) * QT] = jnp.full((1, QT), rb_ref[N_BUCKETS - 1, g * HPG + h] * LOG2E, _F32)

    GQ = G * Q
    row = lax.broadcasted_iota(jnp.int32, (GQ, 1), 0)
    row_g = row // Q
    t = PAST_LEN + row % Q

    def by_group(fn):
        out = fn(0)
        for g in range(1, G):
            out = jnp.where(row_g == g, fn(g), out)
        return out

    def sample_table(o_ref, kpos_fn, valid_fn):
        k = lax.broadcasted_iota(jnp.int32, (1, o_ref.shape[1]), 1)
        dist = t - kpos_fn(k)
        valid = valid_fn(dist, k)
        for h in range(HPG):
            b = by_group(lambda g: _bias_of(dist, rb_ref, g * HPG + h, starts))
            o_ref[h * GQ:(h + 1) * GQ, :] = jnp.where(valid, b, NEG)

    w_buf = swin_o.shape[1]
    sample_table(scmp_o, lambda k: CMP_STRIDE * k + CMP_BLOCK - 1, lambda d, k: (d >= 0) & (k < n_cmp_s))
    sample_table(swin_o, lambda k: PAST_LEN - w_buf + k,
                 lambda d, k: (d >= 0) & (d < WINDOW) & (PAST_LEN - w_buf + k >= 0))
    sample_table(slast_o, lambda k: PAST_LEN - PAGE_SIZE + k, lambda d, k: d >= 0)
    sample_table(snew_o, lambda k: PAST_LEN + k, lambda d, k: (d >= 0) & (k < Q))
    for h in range(HPG):
        far = by_group(lambda g: jnp.full((GQ, LANES), rb_ref[N_BUCKETS - 1, g * HPG + h] * LOG2E, _F32))
        sfar_o[h * GQ:(h + 1) * GQ, :] = far


def _bias_tables(rel_bias, n_chunk_s, n_cmp_s, w_buf):
    G, HPG, QT, Q = NSA_KV_GROUPS, HEADS_PER_GROUP, QUERY_TILE, DEC_SEQ
    NL, R = HPG * QT, HPG * G * Q
    f32 = lambda *s: jax.ShapeDtypeStruct(s, _F32)
    return pl.pallas_call(
        functools.partial(_bias_tables_kernel, n_cmp_s=n_cmp_s),
        in_specs=[pl.BlockSpec(memory_space=pltpu.SMEM)],
        out_shape=[f32(G, CMP_WIN, NL), f32(G, 2 * KEY_TILE, NL), f32(G, WINDOW + QT, NL), f32(G, 1, NL),
                   f32(R, n_chunk_s), f32(R, w_buf), f32(R, PAGE_SIZE), f32(R, PAGE_SIZE), f32(R, LANES)],
        compiler_params=_params(),
        name="bias_tables",
    )(rel_bias)


def kernel(x_prompt, x_sample, cache_cmp, cache_sel, cache_sb, state_win, state_conv, page_table,
           rel_bias, g_pre_mix, w_in, w_ck1, b_ck1, w_ck2, w_cv1, b_cv1, w_cv2, w_branch_a, w_branch_b,
           w_out, g_post_mix, g_pre_ffn, w_ffn_up, conv_w, conv_b, w_ffn_down, g_post_ffn):
    assert DEPTH == 1 and x_prompt.shape[0] == 1
    G, HPG, dh, Q = NSA_KV_GROUPS, HEADS_PER_GROUP, HEAD_DIM, DEC_SEQ
    T = x_prompt.shape[1]
    DB = x_sample.shape[0]
    n_pages = page_table.shape[1]
    w_buf = state_win.shape[2]
    row = lambda v: v.reshape(1, -1)

    w_proj = _split_w_in(w_in[0])
    g_pre = row(g_pre_mix[0])
    w1k, w1v = _bf(_chunk_weights(w_ck1[0])), _bf(_chunk_weights(w_cv1[0]))
    b1k, b1v = row(jnp.tile(b_ck1[0], G)), row(jnp.tile(b_cv1[0], G))
    w2k, w2v = _bf(_group_diag(w_ck2[0])), _bf(_group_diag(w_cv2[0]))
    wa, wb, wo = _bf(w_branch_a[0]), _bf(w_branch_b[0]), _bf(w_out[0])
    wua, wub = _bf(w_ffn_up[0][:, :D_FF]), _bf(w_ffn_up[0][:, D_FF:])
    wd = _bf(w_ffn_down[0])
    cw, cb = conv_w[0], row(conv_b[0])
    g_post_m, g_pre_f, g_post_f = row(g_post_mix[0]), row(g_pre_ffn[0]), row(g_post_ffn[0])
    tt = jnp.asarray(_cumsum_matrix(), _BF16)

    xp = x_prompt[0]
    qa, kv, kvb, gn, qb, sb, sbb, gm = _proj_in(xp, g_pre, w_proj)
    n_chunk = T // CMP_STRIDE
    n_cmp = (T - CMP_BLOCK) // CMP_STRIDE + 1
    kvcol = lambda a, j: a[:, j * KV_WIDTH:(j + 1) * KV_WIDTH]
    kc, vc = _compress_prompt(kvcol(kvb, 0).reshape(n_chunk, CMP_STRIDE * KV_WIDTH),
                              kvcol(kvb, 1).reshape(n_chunk, CMP_STRIDE * KV_WIDTH),
                              w1k, w1v, b1k, b1v, w2k, w2v)
    kcp = jnp.pad(kc, ((CMP_PAD, 0), (0, 0)))
    vcp = jnp.pad(vc, ((CMP_PAD, 0), (0, 0)))
    front = lambda a: jnp.pad(a, ((WINDOW, 0), (0, 0)))
    cpp = PAGE_SIZE // CMP_STRIDE
    n_chunk_s = n_pages * cpp
    n_cmp_s = (PAST_LEN + Q - CMP_BLOCK) // CMP_STRIDE + 1
    bc, bs, bw, bfar, bcmp, bwin, blast, bnew, bfar_s = _bias_tables(rel_bias, n_chunk_s, n_cmp_s, w_buf)
    ct =jnp.asarray(_cover_matrix(n_chunk + CMP_PAD, T // SEL_BLOCK, n_cmp, CMP_PAD).T, _BF16)
    o_a = _nsa_prompt(qa, gn, kcp, vcp, front(kvcol(kvb, 2)), front(kvcol(kvb, 3)),
                      front(kvcol(kvb, 4)), front(kvcol(kvb, 5)), bc, bs, bw, bfar, ct)
    o_b = _sb_prompt(qb, sbb, tt)
    x1, h2 = _merge(xp, o_a, o_b, gm, wa, wb, wo, g_post_m, g_pre_f)
    y_p, tails = _ffn_prompt(h2, x1, wua, wub, cw, cb, wd, g_post_f)

    kv_cmp_p = kv[:, 0:2 * KV_WIDTH].reshape(1, 1, T, 2, G, dh)
    kv_sel_p = kv[:, 2 * KV_WIDTH:4 * KV_WIDTH].reshape(1, 1, T, 2, G, dh)
    kv_sb_p = sb.reshape(1, 1, T, 2, SB_HEADS, dh)
    win_rows = jnp.pad(kv[:, 4 * KV_WIDTH:], ((WINDOW, 0), (0, 0)))[T + WINDOW - w_buf:]
    win_p = win_rows.reshape(1, 1, w_buf, 2, G, dh)
    conv_p = tails[-(CONV_W - 1):].reshape(1, 1, CONV_W - 1, D_FF)

    xs = jnp.transpose(x_sample, (1, 0, 2)).reshape(Q * DB, D_MODEL)
    qa_s, kv_s, _, gn_s, qb_s, sb_s, _, gm_s = _proj_in(xs, g_pre, w_proj)
    by_seq = lambda a: jnp.transpose(a.reshape(Q, DB, -1), (1, 0, 2))
    page_flat = page_table.reshape(-1)

    assert n_cmp_s < n_chunk_s + 1 and Q <= SUBLANES and Q <= CMP_STRIDE
    parts = _cmp_pages(page_flat, _pages_last(cache_cmp[0]), _bf(_page_chunk_weights(w_ck1[0], w_cv1[0])))
    parts = parts.reshape(DB, n_chunk_s, 4 * KV_WIDTH)

    q5 = by_seq(qa_s).reshape(DB, Q, G, HPG, dh)
    q5 = jnp.transpose(q5, (0, 3, 2, 1, 4))
    q32 = jnp.einsum('bhgqd,gk->bhgqkd', q5, jnp.eye(G, dtype=q5.dtype)).reshape(DB, HPG * G * Q, KV_WIDTH)
    g5 = by_seq(gn_s)[:, :, :3 * NSA_HEADS].reshape(DB, Q, 3, G, HPG)
    gate32 = jnp.transpose(g5, (0, 4, 3, 1, 2)).reshape(DB, HPG * G * Q, 3)
    gate32 = jnp.pad(gate32, ((0, 0), (0, 0), (0, LANES - 3)))
    new8 = jnp.pad(by_seq(kv_s), ((0, 0), (0, SUBLANES - Q), (0, 0)))
    n_sel_s = -(-(PAST_LEN + Q) // SEL_BLOCK)
    n_blk_s = -(-n_sel_s // LANES) * LANES
    cmat =jnp.asarray(_cover_matrix(n_chunk_s, n_blk_s, n_cmp_s, 0), _BF16)
    cmat = cmat * (np.arange(n_blk_s)[None, :] < n_sel_s)
    cur_blocks = tuple(int((PAST_LEN + j) // SEL_BLOCK) for j in range(Q))
    o32 = _nsa_sample(page_flat, _pages_last(cache_sel[0]), q32, gate32, parts,
                      _pages_last(state_win[0]), new8, b1k, b1v, w2k, w2v,
                      bcmp, bwin, blast, bnew, bfar_s, cmat, n_cmp_s, n_sel_s, cur_blocks)
    o6 = o32.reshape(DB, HPG, G, Q, G, dh)
    o_a_s = jnp.stack([o6[:, :, g, :, g, :] for g in range(G)], axis=1)
    o_a_s = jnp.transpose(o_a_s, (3, 0, 1, 2, 4)).reshape(Q * DB, NSA_WIDTH)

    qb5 = jnp.transpose(by_seq(qb_s).reshape(DB, Q, SB_HEADS, dh), (0, 2, 1, 3))
    qsb = jnp.einsum('bhqd,hk->bhqkd', qb5, jnp.eye(SB_HEADS, dtype=qb5.dtype)).reshape(DB, SB_HEADS * Q, SB_WIDTH)
    sb_new8 = jnp.pad(by_seq(sb_s), ((0, 0), (0, SUBLANES - Q), (0, 0)))
    rq = np.arange(SB_HEADS * Q)[:, None] % Q
    nmask = jnp.asarray((np.arange(PAGE_SIZE)[None, :] < rq).astype(np.float32))
    osb = _sb_sample(page_flat, _pages_last(cache_sb[0]), qsb, sb_new8, nmask, tt)
    o7 = osb.reshape(DB, SB_HEADS, Q, SB_HEADS, dh)
    o_b_s = jnp.stack([o7[:, h, :, h, :] for h in range(SB_HEADS)], axis=1)
    o_b_s = jnp.transpose(o_b_s, (2, 0, 1, 3)).reshape(Q * DB, SB_WIDTH)

    x1_s, h2_s = _merge(xs, _bf(o_a_s), _bf(o_b_s), gm_s, wa, wb, wo, g_post_m, g_pre_f)
    sc = jnp.transpose(state_conv[0], (1, 0, 2))
    y_s_rows, tail_s = _ffn_sample(h2_s, x1_s, sc, wua, wub, cw, cb, wd, g_post_f)

    y_s = by_seq(y_s_rows)
    kv_seq = by_seq(kv_s)
    kv_cmp_s = kv_seq[:, :, 0:2 * KV_WIDTH].reshape(1, DB, Q, 2, G, dh)
    kv_sel_s = kv_seq[:, :, 2 * KV_WIDTH:4 * KV_WIDTH].reshape(1, DB, Q, 2, G, dh)
    kv_sb_s = by_seq(sb_s).reshape(1, DB, Q, 2, SB_HEADS, dh)
    win_new = kv_seq[:, :, 4 * KV_WIDTH:].reshape(DB, Q, 2, G, dh)
    win_s = jnp.concatenate([state_win[0], win_new], axis=1)[:, -w_buf:][None]
    conv_s = jnp.transpose(tail_s.reshape(CONV_W - 1, DB, D_FF), (1, 0, 2))[None]

    return (y_p[None], y_s, kv_cmp_p, kv_sel_p, kv_sb_p, win_p, conv_p,
            kv_cmp_s, kv_sel_s, kv_sb_s, win_s, conv_s)
```

```python
import functools
import math

import numpy as np
import jax
import jax.numpy as jnp
from jax import lax
from jax.experimental import pallas as pl
from jax.experimental.pallas import tpu as pltpu

D_MODEL = 1024
SEQ = 16384
DEPTH = 1
DEC_BATCH = 128
DEC_SEQ = 4
PAST_LEN = 8192
PAGE_SIZE = 128
HEAD_DIM = 64
NSA_HEADS = 8
NSA_KV_GROUPS = 2
HEADS_PER_GROUP = NSA_HEADS // NSA_KV_GROUPS
SB_HEADS = 8
NSA_WIDTH = NSA_HEADS * HEAD_DIM
SB_WIDTH = SB_HEADS * HEAD_DIM
KV_WIDTH = NSA_KV_GROUPS * HEAD_DIM
CMP_BLOCK = 32
CMP_STRIDE = 16
SEL_BLOCK = 64
SEL_TOP = 16
SEL_COVER_W = (0.5, 1.0, 1.0, 1.0, 0.5)
WINDOW = 512
N_BUCKETS = 32
MAX_DISTANCE = 128
D_FF = 2816
CONV_W = 3
SCALE = HEAD_DIM ** -0.5
LOG2E = math.log2(math.e)
EPS = 1e-6
NEG = -1e30
FORCE_SCORE = 1e9
MASKED_BELOW = -5e29
MASK_PENALTY = -NEG
LANES = 128
SUBLANES = 8
QUERY_TILE = 128
KEY_TILE = 128
ROW_TILE = 256
CMP_PAD = 16
CMP_WIN = 24
PAGES_PER_STEP = 16
SB_QUERY_TILE = 256
SB_KEY_CHUNK = 1024
SEL_KEY_CHUNK = 512
VMEM_LIMIT = 56 * 1024 * 1024

_F32 = jnp.float32
_BF16 = jnp.bfloat16


def _bf(x):
    return x.astype(_BF16)


def _dot(a, b):
    return jnp.dot(a, b, preferred_element_type=_F32)


def _dot_nt(a, b):
    return lax.dot_general(a, b, (((1,), (1,)), ((), ())), preferred_element_type=_F32)


def _dot_tn(a, b):
    return lax.dot_general(a, b, (((0,), (0,)), ((), ())), preferred_element_type=_F32)


def _split_hi_lo(x):
    hi = _bf(x)
    lo = _bf(x - hi.astype(_F32))
    return hi, lo


def _rms(x, g):
    return x * lax.rsqrt(jnp.mean(x * x, axis=-1, keepdims=True) + EPS) * g


def _softplus_log2(z2):
    return jnp.maximum(z2, 0.0) + jnp.log2(1.0 + jnp.exp2(-jnp.abs(z2)))


def _const_spec(shape):
    nd = len(shape)
    return pl.BlockSpec(shape, lambda *_: (0,) * nd)


def _params(sem=None):
    return pltpu.CompilerParams(dimension_semantics=sem, vmem_limit_bytes=VMEM_LIMIT)


def _bucket_np(dist):
    n = np.maximum(dist, 0)
    max_exact = N_BUCKETS // 2
    nf = np.maximum(n, 1).astype(np.float32)
    large = max_exact + (np.log(nf / max_exact) / math.log(MAX_DISTANCE / max_exact)
                         * (N_BUCKETS - max_exact)).astype(np.int32)
    return np.where(n < max_exact, n, np.minimum(large, N_BUCKETS - 1)).astype(np.int32)


def _cover_matrix(n_rows, n_cols, n_cmp, row_offset):
    r = np.arange(n_rows)[:, None] - row_offset
    j = np.arange(n_cols)[None, :]
    k = r - 4 * j + 1
    w = np.asarray(SEL_COVER_W, np.float32)
    ok = (k >= 0) & (k <= 4) & (r >= 0) & (r < n_cmp)
    return np.where(ok, w[np.clip(k, 0, 4)], 0.0).astype(np.float32)


def _cumsum_matrix():
    k = np.arange(LANES)
    tri = (k[:, None] >= k[None, :]).astype(np.float32)
    half = np.concatenate([tri, np.ones((LANES, LANES), np.float32)], axis=1)
    return np.concatenate([half, half], axis=0)


def _proj_in_kernel(x_ref, g_ref, wqa, wkv, wgn, wqb, wsb, wgm,
                    qa_o, kv_o, kvb_o, gn_o, qb_o, sb_o, sbb_o, gm_o):
    h = _bf(_rms(x_ref[...], g_ref[...]))
    qa_o[...] = _bf(_dot(h, wqa[...]) * (SCALE * LOG2E))
    kv = _dot(h, wkv[...])
    kv_o[...] = kv
    kvb_o[...] = _bf(kv)
    gn_o[...] = _dot(h, wgn[...])
    qb_o[...] = _bf(_dot(h, wqb[...]) * (SCALE * LOG2E))
    sb = _dot(h, wsb[...])
    sb_o[...] = sb
    sbb_o[...] = _bf(sb)
    gm_o[...] = _dot(h, wgm[...])


def _proj_in(x, g, ws):
    rows = x.shape[0]
    tm = min(ROW_TILE, rows)
    widths = [w.shape[1] for w in ws]
    out_shape = [
        jax.ShapeDtypeStruct((rows, widths[0]), _BF16),
        jax.ShapeDtypeStruct((rows, widths[1]), _F32),
        jax.ShapeDtypeStruct((rows, widths[1]), _BF16),
        jax.ShapeDtypeStruct((rows, widths[2]), _F32),
        jax.ShapeDtypeStruct((rows, widths[3]), _BF16),
        jax.ShapeDtypeStruct((rows, widths[4]), _F32),
        jax.ShapeDtypeStruct((rows, widths[4]), _BF16),
        jax.ShapeDtypeStruct((rows, widths[5]), _F32),
    ]
    row_spec = lambda w: pl.BlockSpec((tm, w), lambda i: (i, 0))
    return pl.pallas_call(
        _proj_in_kernel,
        grid=(rows // tm,),
        in_specs=[row_spec(D_MODEL), _const_spec((1, D_MODEL))] + [_const_spec(w.shape) for w in ws],
        out_specs=[row_spec(s.shape[1]) for s in out_shape],
        out_shape=out_shape,
        compiler_params=_params(("arbitrary",)),
        name="proj_in",
    )(x, g, *ws)


def _compress_finish(parts_lo, parts_hi, b1, w2):
    n = parts_hi.shape[0]
    hi_next = pltpu.roll(parts_hi, n - 1, 0)
    hid = parts_lo + hi_next + b1
    return _dot(_bf(jax.nn.gelu(hid)), w2)


def _compress_kernel(ck_ref, cv_ref, w1k, w1v, b1k, b1v, w2k, w2v, kc_o, vc_o):
    for c_ref, w1, b1, w2, o in ((ck_ref, w1k, b1k, w2k, kc_o), (cv_ref, w1v, b1v, w2v, vc_o)):
        parts = _dot(c_ref[...], w1[...])
        o[...] = _bf(_compress_finish(parts[:, :KV_WIDTH], parts[:, KV_WIDTH:], b1[...], w2[...]))


def _compress_prompt(ck, cv, w1k, w1v, b1k, b1v, w2k, w2v):
    n = ck.shape[0]
    return pl.pallas_call(
        _compress_kernel,
        out_shape=[jax.ShapeDtypeStruct((n, KV_WIDTH), _BF16)] * 2,
        compiler_params=_params(),
        name="compress_prompt",
    )(ck, cv, w1k, w1v, b1k, b1v, w2k, w2v)


def _chunk_weights(w1):
    G, dh = NSA_KV_GROUPS, HEAD_DIM
    w1r = w1.reshape(2, CMP_STRIDE, dh, dh)
    eye = jnp.eye(G, dtype=w1.dtype)
    big = jnp.einsum('slde,gh->lgdshe', w1r, eye)
    return big.reshape(CMP_STRIDE * G * dh, 2 * G * dh)


def _group_diag(w2):
    return jnp.kron(jnp.eye(NSA_KV_GROUPS, dtype=w2.dtype), w2)


def _nsa_prompt_kernel(qa_ref, gn_ref, kcp_ref, vcp_ref, ks_ref, vs_ref, kw_ref, vw_ref,
                       bc_ref, bs_ref, bw_ref, bfar_ref, bfl_ref, ct_ref, pat_ref,
                       oa_ref, s_scr, sel_scr, q_scr, selm_scr, sa_scr, sb_scr):
    G, HPG, dh = NSA_KV_GROUPS, HEADS_PER_GROUP, HEAD_DIM
    QT = QUERY_TILE
    NL = HPG * QT
    qb = pl.program_id(0)
    c = qb * QT
    n_cp = kcp_ref.shape[0]
    n_blk = ct_ref.shape[0]

    lane_half = lax.broadcasted_iota(jnp.int32, (1, LANES), 1) // dh
    gates_t = jnp.transpose(jax.nn.sigmoid(gn_ref[...]))

    def gate_row(branch, g):
        base = branch * NSA_HEADS + g * HPG
        return jnp.concatenate([gates_t[base + h:base + h + 1, :] for h in range(HPG)], axis=1)

    ti = c + lax.broadcasted_iota(jnp.int32, (1, QT), 1)
    cur = ti // SEL_BLOCK

    for g in range(G):
        q_rows = []
        for h in range(HPG):
            col = (g * HPG + h) * dh
            tile = qa_ref[:, (col // LANES) * LANES:(col // LANES + 1) * LANES].astype(_F32)
            if (col % LANES) // dh != g:
                tile = pltpu.roll(tile, dh, 1)
            q_rows.append(_bf(jnp.where(lane_half == g, tile, 0.0)))
        q_scr[g] = jnp.concatenate(q_rows, axis=0)

    o_cmp = [None] * G
    for g in range(G):
        qg = q_scr[g]
        bfar = bfar_ref[g]

        r0 = pl.multiple_of(qb * (QT // CMP_STRIDE), SUBLANES)
        s_scr[...] = _dot_nt(kcp_ref[...], qg) + bfar
        s_scr[pl.ds(r0, CMP_WIN), :] = s_scr[pl.ds(r0, CMP_WIN), :] - bfar + bc_ref[g]
        row = lax.broadcasted_iota(jnp.int32, (n_cp, 1), 0)
        s = jnp.where((row >= CMP_PAD) & (row < r0 + CMP_WIN), s_scr[...], NEG)
        m = jnp.max(s, axis=0, keepdims=True)
        e = jnp.exp2(s - m)
        p = jnp.where(s > MASKED_BELOW, e / jnp.sum(e, axis=0, keepdims=True), 0.0)
        o_cmp[g] = _dot_tn(vcp_ref[...], _bf(p))[g * dh:(g + 1) * dh]
        p_grp = p[:, 0:QT]
        for h in range(1, HPG):
            p_grp = p_grp + p[:, h * QT:(h + 1) * QT]
        p_hi, p_lo = _split_hi_lo(p_grp)
        p_slc = _dot(ct_ref[...], p_hi) + _dot(ct_ref[...], p_lo)

        blk = lax.broadcasted_iota(jnp.int32, (n_blk, 1), 0)
        blk_f = blk.astype(_F32)
        forced = (blk == 0) | (blk == cur) | (blk == cur - 1)
        score = jnp.where(forced, FORCE_SCORE, jnp.where(blk <= cur, p_slc, NEG))
        sel = jnp.zeros((n_blk, QT), _F32)
        for _ in range(min(SEL_TOP, n_blk)):
            best = jnp.max(score, axis=0, keepdims=True)
            first = jnp.min(jnp.where(score == best, blk_f, float(n_blk)), axis=0, keepdims=True)
            hit = blk_f == first
            sel = jnp.where(hit, 1.0, sel)
            score = jnp.where(hit, -jnp.inf, score)
        sel_scr[g] = jnp.concatenate([sel] * HPG, axis=1)
        selm_scr[g] = _bf((jnp.transpose(sel) - 1.0) * MASK_PENALTY)

    def sel_rows(g, first_blk, n):
        rows = [jnp.broadcast_to(sel_scr[g, pl.ds(first_blk + j, 1), :], (SEL_BLOCK, NL)) for j in range(n)]
        return jnp.concatenate(rows, axis=0) > 0.5

    def sel_step(g, vals, s_tile, mask, carry, zero_masked):
        m_run, l_run, acc = carry
        if mask is not None:
            s_tile = jnp.where(mask, s_tile, NEG)
        m_new = jnp.maximum(m_run, jnp.max(s_tile, axis=0, keepdims=True))
        alpha = jnp.exp2(m_run - m_new)
        p_t = jnp.exp2(s_tile - m_new)
        if zero_masked:
            p_t = jnp.where(mask, p_t, 0.0)
        l_new = alpha * l_run + jnp.sum(p_t, axis=0, keepdims=True)
        pv = _dot_tn(vals, _bf(p_t))[g * dh:(g + 1) * dh]
        return m_new, l_new, alpha * acc + pv

    n_far_blk = jnp.maximum(qb - 1, 0) * (KEY_TILE // SEL_BLOCK)
    blk_per_chunk = SEL_KEY_CHUNK // SEL_BLOCK
    blk_col = lax.broadcasted_iota(jnp.int32, (n_blk, 1), 0)
    lane = lax.broadcasted_iota(jnp.int32, (1, LANES), 1)

    n_far_chunks = (n_far_blk + blk_per_chunk - 1) // blk_per_chunk

    def chunk_rows(f):
        return pl.multiple_of(f * SEL_KEY_CHUNK + WINDOW, KEY_TILE)

    def far_scores(f, out_scr):
        rows = chunk_rows(jnp.minimum(f, jnp.maximum(n_far_chunks - 1, 0)))
        keys = jnp.concatenate([ks_ref[pl.ds(rows, SEL_KEY_CHUNK), :], pat_ref[...]], axis=1)
        blk_of_lane = f * blk_per_chunk + lane
        pick = _bf(jnp.where((blk_col == blk_of_lane) & (lane < blk_per_chunk), 1.0, 0.0))
        off = jnp.where((lane < blk_per_chunk) & (blk_of_lane >= n_far_blk), -MASK_PENALTY, 0.0)
        pens = [_dot(selm_scr[g], pick) + off for g in range(G)]
        for g in range(G):
            extra = jnp.concatenate([_bf(pens[g] + bfl_ref[g, h:h + 1, :]) for h in range(HPG)], axis=0)
            q_aug = jnp.concatenate([q_scr[g], extra], axis=1)
            out_scr[g] = _dot_nt(keys, q_aug)

    def far_consume(f, in_scr, state):
        vals = vs_ref[pl.ds(chunk_rows(jnp.minimum(f, jnp.maximum(n_far_chunks - 1, 0))), SEL_KEY_CHUNK), :]
        return tuple(sel_step(g, vals, in_scr[g], None, state[g], False) for g in range(G))

    def far_body(k, state):
        far_scores(2 * k + 1, sb_scr)
        state = far_consume(2 * k, sa_scr, state)
        far_scores(2 * k + 2, sa_scr)
        return far_consume(2 * k + 1, sb_scr, state)

    init = tuple((jnp.full((1, NL), NEG, _F32), jnp.zeros((1, NL), _F32), jnp.zeros((dh, NL), _F32))
                 for _ in range(G))
    far_scores(0, sa_scr)
    far = lax.fori_loop(0, (n_far_chunks + 1) // 2, far_body, init)

    for g in range(G):
        qg = q_scr[g]
        rn = pl.multiple_of(c + WINDOW - KEY_TILE, KEY_TILE)
        keys = ks_ref[pl.ds(rn, 2 * KEY_TILE), :]
        vals = vs_ref[pl.ds(rn, 2 * KEY_TILE), :]
        bias_near = bs_ref[g]
        s_tile = _dot_nt(keys, qg) + bias_near
        kpos = c - KEY_TILE + lax.broadcasted_iota(jnp.int32, (2 * KEY_TILE, 1), 0)
        first_blk = jnp.maximum(qb - 1, 0) * (KEY_TILE // SEL_BLOCK)
        near_sel = jnp.concatenate([
            jnp.where(qb > 0, 1.0, 0.0) * sel_rows(g, first_blk, KEY_TILE // SEL_BLOCK).astype(_F32),
            sel_rows(g, qb * (KEY_TILE // SEL_BLOCK), KEY_TILE // SEL_BLOCK).astype(_F32)], axis=0) > 0.5
        mask = near_sel & (bias_near > MASKED_BELOW) & (kpos >= 0)
        _, l_fin, acc = sel_step(g, vals, s_tile, mask, far[g], True)
        o_sel = acc / jnp.where(l_fin > 0.0, l_fin, 1.0)

        rw = pl.multiple_of(c, KEY_TILE)
        n_w = WINDOW + QT
        bias_w = bw_ref[g]
        s_w = _dot_nt(kw_ref[pl.ds(rw, n_w), :], qg) + bias_w
        kpos_w = c - WINDOW + lax.broadcasted_iota(jnp.int32, (n_w, 1), 0)
        mask_w = (bias_w > MASKED_BELOW) & (kpos_w >= 0)
        s_w = jnp.where(mask_w, s_w, NEG)
        e_w = jnp.exp2(s_w - jnp.max(s_w, axis=0, keepdims=True))
        p_w = jnp.where(mask_w, e_w / jnp.sum(e_w, axis=0, keepdims=True), 0.0)
        o_win = _dot_tn(vw_ref[pl.ds(rw, n_w), :], _bf(p_w))[g * dh:(g + 1) * dh]

        o_t = gate_row(0, g) * o_cmp[g] + gate_row(1, g) * o_sel + gate_row(2, g) * o_win
        for pair in range(HPG // 2):
            blk_t = jnp.concatenate([o_t[:, (2 * pair) * QT:(2 * pair + 1) * QT],
                                     o_t[:, (2 * pair + 1) * QT:(2 * pair + 2) * QT]], axis=0)
            col = (g * HPG + 2 * pair) * dh
            oa_ref[:, col:col + LANES] = _bf(jnp.transpose(blk_t))


def _chunk_pattern():
    nb = SEL_KEY_CHUNK // SEL_BLOCK
    pat = np.zeros((SEL_KEY_CHUNK, LANES), np.float32)
    pat[np.arange(SEL_KEY_CHUNK), np.arange(SEL_KEY_CHUNK) // SEL_BLOCK] = 1.0
    pat[:, nb:nb + 2] = 1.0
    return pat


def _nsa_prompt(qa, gn, kcp, vcp, ks, vs, kw, vw, bc, bs, bw, bfar, bfl, ct):
    T = qa.shape[0]
    QT = QUERY_TILE
    NL = HEADS_PER_GROUP * QT
    pat = jnp.asarray(_chunk_pattern(), _BF16)
    full = lambda a: _const_spec(a.shape)
    return pl.pallas_call(
        _nsa_prompt_kernel,
        grid=(T // QT,),
        in_specs=[pl.BlockSpec((QT, NSA_WIDTH), lambda i: (i, 0)),
                  pl.BlockSpec((QT, LANES), lambda i: (i, 0)),
                  full(kcp), full(vcp), full(ks), full(vs), full(kw), full(vw),
                  full(bc), full(bs), full(bw), full(bfar), full(bfl), full(ct), full(pat)],
        out_specs=pl.BlockSpec((QT, NSA_WIDTH), lambda i: (i, 0)),
        out_shape=jax.ShapeDtypeStruct((T, NSA_WIDTH), _BF16),
        scratch_shapes=[pltpu.VMEM((kcp.shape[0], NL), _F32),
                        pltpu.VMEM((NSA_KV_GROUPS, ct.shape[0], NL), _F32),
                        pltpu.VMEM((NSA_KV_GROUPS, NL, LANES), _BF16),
                        pltpu.VMEM((NSA_KV_GROUPS, QT, ct.shape[0]), _BF16),
                        pltpu.VMEM((NSA_KV_GROUPS, SEL_KEY_CHUNK, NL), _F32),
                        pltpu.VMEM((NSA_KV_GROUPS, SEL_KEY_CHUNK, NL), _F32)],
        compiler_params=_params(("arbitrary",)),
        name="nsa_prompt",
    )(qa, gn, kcp, vcp, ks, vs, kw, vw, bc, bs, bw, bfar, bfl, ct, pat)


def _sb_chunk(z, carry, tt, mask, latest_first):
    nb = z.shape[1] // LANES
    sp = _softplus_log2(z)
    if mask is not None:
        sp = jnp.where(mask, sp, 0.0)
    hi, lo = _split_hi_lo(sp)
    a_blocks = [None] * nb
    for j in (range(nb) if latest_first else reversed(range(nb))):
        sl = slice(j * LANES, (j + 1) * LANES)
        cs = _dot(jnp.concatenate([hi[:, sl], lo[:, sl]], axis=1), tt)
        a_blocks[j] = jnp.exp2(z[:, sl] - cs[:, :LANES] - carry)
        carry = carry + cs[:, LANES:]
    a = a_blocks[0] if nb == 1 else jnp.concatenate(a_blocks, axis=1)
    if mask is not None:
        a = jnp.where(mask, a, 0.0)
    return a, carry


def _sb_prompt_kernel(q_ref, k_ref, v_ref, tt_ref, o_ref, acc_scr, car_scr):
    dh = HEAD_DIM
    QT = q_ref.shape[0]
    KW = min(SB_KEY_CHUNK, k_ref.shape[0])
    i = pl.program_id(1)
    lane_half = lax.broadcasted_iota(jnp.int32, (1, LANES), 1) // dh
    q = q_ref[...].astype(_F32)
    qh = [_bf(jnp.where(lane_half == h, q, 0.0)) for h in range(2)]
    tt = tt_ref[...]
    last = (i * QT + QT - 1) // KW

    def chunk(ci, mask):
        r = pl.multiple_of(ci * KW, KW)
        keys = k_ref[pl.ds(r, KW), :]
        vals = v_ref[pl.ds(r, KW), :]
        for h in range(2):
            a, carry = _sb_chunk(_dot_nt(qh[h], keys), car_scr[h], tt, mask, latest_first=False)
            acc_scr[h] = acc_scr[h] + _dot(_bf(a), vals)
            car_scr[h] = carry

    acc_scr[...] = jnp.zeros(acc_scr.shape, _F32)
    car_scr[...] = jnp.zeros(car_scr.shape, _F32)
    t = i * QT + lax.broadcasted_iota(jnp.int32, (QT, 1), 0)
    kpos = last * KW + lax.broadcasted_iota(jnp.int32, (1, KW), 1)
    chunk(last, kpos < t)

    def body(n, _):
        chunk(last - 1 - n, None)
        return 0

    lax.fori_loop(0, last, body, 0)
    o_ref[...] = _bf(jnp.where(lane_half == 0, acc_scr[0], acc_scr[1]))


def _sb_prompt(qb, sbb, tt):
    T = qb.shape[0]
    QT = min(SB_QUERY_TILE, T)
    n_pair = SB_WIDTH // LANES
    return pl.pallas_call(
        _sb_prompt_kernel,
        grid=(n_pair, T // QT),
        in_specs=[pl.BlockSpec((QT, LANES), lambda p, i: (i, p)),
                  pl.BlockSpec((T, LANES), lambda p, i: (0, p)),
                  pl.BlockSpec((T, LANES), lambda p, i: (0, n_pair + p)),
                  _const_spec(tt.shape)],
        out_specs=pl.BlockSpec((QT, LANES), lambda p, i: (i, p)),
        out_shape=jax.ShapeDtypeStruct((T, SB_WIDTH), _BF16),
        scratch_shapes=[pltpu.VMEM((2, QT, LANES), _F32), pltpu.VMEM((2, QT, LANES), _F32)],
        compiler_params=_params(("arbitrary", "arbitrary")),
        name="sb_prompt",
    )(qb, sbb, sbb, tt)


def _merge_kernel(x_ref, oa_ref, ob_ref, gm_ref, wa, wb, wo, g_post, g_pre, x1_o, h2_o):
    gm = gm_ref[...]
    mixed = (jax.nn.sigmoid(gm[:, :D_MODEL]) * _dot(oa_ref[...], wa[...])
             + jax.nn.sigmoid(gm[:, D_MODEL:]) * _dot(ob_ref[...], wb[...]))
    x1 = x_ref[...] + _rms(_dot(_bf(mixed), wo[...]), g_post[...])
    x1_o[...] = x1
    h2_o[...] = _bf(_rms(x1, g_pre[...]))


def _merge(x, oa, ob, gm, wa, wb, wo, g_post, g_pre):
    rows = x.shape[0]
    tm = min(ROW_TILE, rows)
    row_spec = lambda w: pl.BlockSpec((tm, w), lambda i: (i, 0))
    return pl.pallas_call(
        _merge_kernel,
        grid=(rows // tm,),
        in_specs=[row_spec(D_MODEL), row_spec(NSA_WIDTH), row_spec(SB_WIDTH), row_spec(2 * D_MODEL),
                  _const_spec(wa.shape), _const_spec(wb.shape), _const_spec(wo.shape),
                  _const_spec((1, D_MODEL)), _const_spec((1, D_MODEL))],
        out_specs=[row_spec(D_MODEL), row_spec(D_MODEL)],
        out_shape=[jax.ShapeDtypeStruct((rows, D_MODEL), _F32), jax.ShapeDtypeStruct((rows, D_MODEL), _BF16)],
        compiler_params=_params(("arbitrary",)),
        name="merge",
    )(x, oa, ob, gm, wa, wb, wo, g_post, g_pre)


def _ffn_tail(x1, conv, gate, wd, g_post):
    f = jax.nn.gelu(conv) * gate
    return x1 + _rms(_dot(_bf(f), wd), g_post)


def _ffn_prompt_kernel(h2_ref, halo_ref, x1_ref, wua, wub, cw_ref, cb_ref, wd, g_post, y_o, tail_o):
    i = pl.program_id(0)
    tm = h2_ref.shape[0]
    h2 = h2_ref[...]
    a_ext = _dot(jnp.concatenate([halo_ref[...], h2], axis=0), wua[...])
    row = lax.broadcasted_iota(jnp.int32, (SUBLANES + tm, 1), 0)
    a_ext = jnp.where((row < SUBLANES) & (i == 0), 0.0, a_ext)
    a0 = a_ext[SUBLANES:]
    a1 = pltpu.roll(a_ext, 1, 0)[SUBLANES:]
    a2 = pltpu.roll(a_ext, 2, 0)[SUBLANES:]
    cw = cw_ref[...]
    conv = cb_ref[...] + cw[0:1] * a2 + cw[1:2] * a1 + cw[2:3] * a0
    y_o[...] = _ffn_tail(x1_ref[...], conv, _dot(h2, wub[...]), wd[...], g_post[...])
    tail_o[...] = a0[tm - SUBLANES:]


def _ffn_prompt(h2, x1, wua, wub, cw, cb, wd, g_post):
    T = h2.shape[0]
    tm = min(ROW_TILE, T)
    hb = tm // SUBLANES
    return pl.pallas_call(
        _ffn_prompt_kernel,
        grid=(T // tm,),
        in_specs=[pl.BlockSpec((tm, D_MODEL), lambda i: (i, 0)),
                  pl.BlockSpec((SUBLANES, D_MODEL), lambda i: (jnp.maximum(i * hb - 1, 0), 0)),
                  pl.BlockSpec((tm, D_MODEL), lambda i: (i, 0)),
                  _const_spec(wua.shape), _const_spec(wub.shape), _const_spec(cw.shape),
                  _const_spec(cb.shape), _const_spec(wd.shape), _const_spec((1, D_MODEL))],
        out_specs=[pl.BlockSpec((tm, D_MODEL), lambda i: (i, 0)),
                   pl.BlockSpec((SUBLANES, D_FF), lambda i: (i, 0))],
        out_shape=[jax.ShapeDtypeStruct((T, D_MODEL), _F32),
                   jax.ShapeDtypeStruct((T // tm * SUBLANES, D_FF), _F32)],
        compiler_params=_params(("arbitrary",)),
        name="ffn_prompt",
    )(h2, h2, x1, wua, wub, cw, cb, wd, g_post)


def _ffn_sample_kernel(h2_ref, x1_ref, sc_ref, wua, wub, cw_ref, cb_ref, wd, g_post, y_o, tail_o):
    db = sc_ref.shape[1]
    nq = h2_ref.shape[0] // db
    h2 = h2_ref[...]
    a0 = _dot(h2, wua[...])
    a1 = jnp.concatenate([sc_ref[1], a0[:(nq - 1) * db]], axis=0)
    a2 = jnp.concatenate([sc_ref[0], sc_ref[1], a0[:(nq - 2) * db]], axis=0)
    cw = cw_ref[...]
    conv = cb_ref[...] + cw[0:1] * a2 + cw[1:2] * a1 + cw[2:3] * a0
    y_o[...] = _ffn_tail(x1_ref[...], conv, _dot(h2, wub[...]), wd[...], g_post[...])
    tail_o[...] = a0[(nq - 2) * db:]


def _ffn_sample(h2, x1, sc, wua, wub, cw, cb, wd, g_post):
    rows = h2.shape[0]
    db = sc.shape[1]
    return pl.pallas_call(
        _ffn_sample_kernel,
        out_shape=[jax.ShapeDtypeStruct((rows, D_MODEL), _F32),
                   jax.ShapeDtypeStruct((2 * db, D_FF), _F32)],
        compiler_params=_params(),
        name="ffn_sample",
    )(h2, x1, sc, wua, wub, cw, cb, wd, g_post)


def _cmp_pages_kernel(pt_ref, *refs):
    n = len(refs) - 3
    w_ref, o_ref, x_scr = refs[n:]
    cpp = PAGE_SIZE // CMP_STRIDE
    pitch = x_scr.shape[1] // CMP_STRIDE
    for kv in range(2):
        for j, r in enumerate(refs[:n]):
            x = jnp.transpose(r[0, kv])
            for c in range(cpp):
                x_scr[kv, pl.ds(j * cpp + c, CMP_STRIDE, stride=pitch), :] = x[c * CMP_STRIDE:(c + 1) * CMP_STRIDE]
        acc = jnp.zeros((n * cpp, 2 * KV_WIDTH), _F32)
        for l in range(CMP_STRIDE):
            acc = acc + _dot(_bf(x_scr[kv, l * pitch:l * pitch + n * cpp, :]), w_ref[l, kv])
        o_ref[:, kv * 2 * KV_WIDTH:(kv + 1) * 2 * KV_WIDTH] = acc


def _slab_pitch(rows):
    tiles = -(-rows // SUBLANES)
    return SUBLANES * (tiles if tiles % 2 else tiles + 1)


def _cmp_pages(page_flat, cache_t, w):
    n_used = page_flat.shape[0]
    cpp = PAGE_SIZE // CMP_STRIDE
    pps = min(PAGES_PER_STEP, n_used)
    page_spec = lambda j: pl.BlockSpec((1,) + cache_t.shape[1:], lambda i, pt: (pt[i * pps + j], 0, 0, 0))
    return pl.pallas_call(
        _cmp_pages_kernel,
        grid_spec=pltpu.PrefetchScalarGridSpec(
            num_scalar_prefetch=1,
            grid=(n_used // pps,),
            in_specs=[page_spec(j) for j in range(pps)] + [pl.BlockSpec(w.shape, lambda i, pt: (0, 0, 0, 0))],
            out_specs=pl.BlockSpec((pps * cpp, 4 * KV_WIDTH), lambda i, pt: (i, 0)),
            scratch_shapes=[pltpu.VMEM((2, CMP_STRIDE * _slab_pitch(pps * cpp), KV_WIDTH), _F32)],
        ),
        out_shape=jax.ShapeDtypeStruct((n_used * cpp, 4 * KV_WIDTH), _F32),
        compiler_params=_params(("arbitrary",)),
        name="cmp_pages",
    )(page_flat, *([cache_t] * pps), w)


def _page_chunk_weights(w1k, w1v):
    G, dh = NSA_KV_GROUPS, HEAD_DIM
    w = jnp.stack([w1k.reshape(2, CMP_STRIDE, dh, dh), w1v.reshape(2, CMP_STRIDE, dh, dh)])
    big = jnp.einsum('ksLde,gh->Lkgdshe', w, jnp.eye(G, dtype=w.dtype))
    return big.reshape(CMP_STRIDE, 2, G * dh, 2 * G * dh)


def _pages_last(cache):
    n, p, kv, h, d = cache.shape
    return jnp.transpose(cache, (0, 2, 3, 4, 1)).reshape(n, kv, h * d, p)


def _softmax_rows(parts):
    m = parts[0].max(axis=1, keepdims=True)
    for s in parts[1:]:
        m = jnp.maximum(m, s.max(axis=1, keepdims=True))
    es = [jnp.exp2(s - m) for s in parts]
    l = es[0].sum(axis=1, keepdims=True)
    for e in es[1:]:
        l = l + e.sum(axis=1, keepdims=True)
    return [jnp.where(s > MASKED_BELOW, e / l, 0.0) for s, e in zip(parts, es)]


def _nsa_sample_kernel(pt_ref, *refs, n_cmp, n_sel, cur_blocks):
    pps = len(refs) - 23
    pages = refs[:pps]
    (q_ref, gate_ref, parts_ref, win_ref, new_ref, b1k, b1v, w2k, w2v,
     bcmp_ref, bwin_ref, blast_ref, bnew_ref, bfar_ref, c_ref, e_ref,
     o_ref, sel_scr, oc_scr, ow_scr, m_scr, l_scr, acc_scr) = refs[pps:]
    G, HPG, dh = NSA_KV_GROUPS, HEADS_PER_GROUP, HEAD_DIM
    R = q_ref.shape[1]
    GQ = R // HPG
    st = pl.program_id(1)
    n_st = pl.num_programs(1)
    q = q_ref[0]
    n_blk = c_ref.shape[0]
    nk = pps * PAGE_SIZE

    @pl.when(st == 0)
    def _():
        parts = parts_ref[0]
        kc = _bf(_compress_finish(parts[:, 0:KV_WIDTH], parts[:, KV_WIDTH:2 * KV_WIDTH], b1k[...], w2k[...]))
        vc = _bf(_compress_finish(parts[:, 2 * KV_WIDTH:3 * KV_WIDTH], parts[:, 3 * KV_WIDTH:], b1v[...], w2v[...]))
        (p,) = _softmax_rows([_dot_nt(q, kc) + bcmp_ref[...]])
        oc_scr[...] = _dot(_bf(p), vc)
        p_grp = p[0:GQ]
        for h in range(1, HPG):
            p_grp = p_grp + p[h * GQ:(h + 1) * GQ]
        p_pad = jnp.concatenate([p_grp, jnp.zeros((LANES - GQ, p_grp.shape[1]), _F32)], axis=0)
        p_hi, p_lo = _split_hi_lo(p_pad)
        p_slc = _dot_nt(c_ref[...], p_hi) + _dot_nt(c_ref[...], p_lo)

        blk = lax.broadcasted_iota(jnp.int32, (n_blk, 1), 0)
        blk_f = blk.astype(_F32)
        cur = jnp.zeros((1, LANES), jnp.int32)
        qi = lax.broadcasted_iota(jnp.int32, (1, LANES), 1) % (GQ // G)
        for j, cb in enumerate(cur_blocks):
            cur = jnp.where(qi == j, cb, cur)
        forced = (blk == 0) | (blk == cur) | (blk == cur - 1)
        score = jnp.where(forced, FORCE_SCORE, jnp.where(blk <= cur, p_slc, NEG))
        score = jnp.where(blk < n_sel, score, -jnp.inf)
        sel = jnp.zeros((n_blk, LANES), _F32)
        for _ in range(min(SEL_TOP, n_sel)):
            best = jnp.max(score, axis=0, keepdims=True)
            first = jnp.min(jnp.where(score == best, blk_f, float(n_blk)), axis=0, keepdims=True)
            hit = blk_f == first
            sel = jnp.where(hit, 1.0, sel)
            score = jnp.where(hit, -jnp.inf, score)
        sel_scr[...] = sel

        new = new_ref[0]
        pad = jnp.zeros((PAGE_SIZE - new.shape[0], KV_WIDTH), _F32)
        col = lambda j: _bf(jnp.concatenate([new[:, j * KV_WIDTH:(j + 1) * KV_WIDTH], pad], axis=0))
        ks_new, vs_new, kw_new, vw_new = col(2), col(3), col(4), col(5)

        p_w, p_wn = _softmax_rows([_dot(q, _bf(win_ref[0, 0])) + bwin_ref[...],
                                   _dot_nt(q, kw_new) + bnew_ref[...]])
        ow_scr[...] = _dot_nt(_bf(p_w), _bf(win_ref[0, 1])) + _dot(_bf(p_wn), vw_new)

        s_new = _dot_nt(q, ks_new) + bnew_ref[...]
        m0 = s_new.max(axis=1, keepdims=True)
        p_new = jnp.where(s_new > MASKED_BELOW, jnp.exp2(s_new - m0), 0.0)
        m_scr[...] = m0
        l_scr[...] = p_new.sum(axis=1, keepdims=True)
        acc_scr[...] = _dot(_bf(p_new), vs_new)

    keys_t = jnp.concatenate([_bf(r[0, 0]) for r in pages], axis=1)
    vals_t = jnp.concatenate([_bf(r[0, 1]) for r in pages], axis=1)
    bfar = bfar_ref[...]
    is_last = st == n_st - 1
    bias = jnp.concatenate([bfar] * (pps - 1) + [jnp.where(is_last, blast_ref[...], bfar)], axis=1)
    s = _dot(q, keys_t) + bias
    blk_per_step = nk // SEL_BLOCK
    sel_st = _bf(sel_scr[pl.ds(pl.multiple_of(st * blk_per_step, SUBLANES), blk_per_step), :])
    mask_gq = _dot_tn(sel_st, e_ref[...])[0:GQ]
    mask = jnp.concatenate([mask_gq] * HPG, axis=0) > 0.5
    s = jnp.where(mask, s, NEG)
    m_new = jnp.maximum(m_scr[...], s.max(axis=1, keepdims=True))
    alpha = jnp.exp2(m_scr[...] - m_new)
    p = jnp.where(mask, jnp.exp2(s - m_new), 0.0)
    l_scr[...] = alpha * l_scr[...] + p.sum(axis=1, keepdims=True)
    acc_scr[...] = alpha * acc_scr[...] + _dot_nt(_bf(p), vals_t)
    m_scr[...] = m_new

    @pl.when(is_last)
    def _():
        gt = jax.nn.sigmoid(gate_ref[0])
        l_fin = l_scr[...]
        o_sel = acc_scr[...] / jnp.where(l_fin > 0.0, l_fin, 1.0)
        o_ref[0] = gt[:, 0:1] * oc_scr[...] + gt[:, 1:2] * o_sel + gt[:, 2:3] * ow_scr[...]


def _nsa_sample(page_flat, cache, q32, gate32, parts, win, new8, b1k, b1v, w2k, w2v,
                bcmp, bwin, blast, bnew, bfar, cmat, n_cmp, n_sel, cur_blocks):
    db, R, _ = q32.shape
    n_pages = page_flat.shape[0] // db
    pps = min(PAGES_PER_STEP, n_pages)
    n_st = n_pages // pps
    page_spec = lambda j: pl.BlockSpec((1,) + cache.shape[1:],
                                       lambda b, s, pt: (pt[b * n_pages + s * pps + j], 0, 0, 0))
    per_seq = lambda a: pl.BlockSpec((1,) + a.shape[1:], lambda b, s, pt: (b,) + (0,) * (a.ndim - 1))
    const = lambda a: pl.BlockSpec(a.shape, lambda b, s, pt: (0,) * a.ndim)
    kern = functools.partial(_nsa_sample_kernel, n_cmp=n_cmp, n_sel=n_sel, cur_blocks=cur_blocks)
    nk = pps * PAGE_SIZE
    expand = jnp.asarray(np.arange(nk // SEL_BLOCK)[:, None] == np.arange(nk)[None, :] // SEL_BLOCK, _BF16)
    return pl.pallas_call(
        kern,
        grid_spec=pltpu.PrefetchScalarGridSpec(
            num_scalar_prefetch=1,
            grid=(db, n_st),
            in_specs=[page_spec(j) for j in range(pps)]
            + [per_seq(q32), per_seq(gate32), per_seq(parts), per_seq(win), per_seq(new8),
               const(b1k), const(b1v), const(w2k), const(w2v),
               const(bcmp), const(bwin), const(blast), const(bnew), const(bfar), const(cmat), const(expand)],
            out_specs=pl.BlockSpec((1, R, KV_WIDTH), lambda b, s, pt: (b, 0, 0)),
            scratch_shapes=[pltpu.VMEM((cmat.shape[0], LANES), _F32),
                            pltpu.VMEM((R, KV_WIDTH), _F32), pltpu.VMEM((R, KV_WIDTH), _F32),
                            pltpu.VMEM((R, 1), _F32), pltpu.VMEM((R, 1), _F32), pltpu.VMEM((R, KV_WIDTH), _F32)],
        ),
        out_shape=jax.ShapeDtypeStruct((db, R, KV_WIDTH), _F32),
        compiler_params=_params(("arbitrary", "arbitrary")),
        name="nsa_sample",
    )(page_flat, *([cache] * pps), q32, gate32, parts, win, new8, b1k, b1v, w2k, w2v,
      bcmp, bwin, blast, bnew, bfar, cmat, expand)


def _sb_sample_kernel(pt_ref, *refs):
    pps = len(refs) - 7
    pages = refs[:pps]
    q_ref, new_ref, nmask_ref, tt_ref, o_ref, acc_scr, car_scr = refs[pps:]
    st = pl.program_id(1)
    q = q_ref[0]
    tt = tt_ref[...]

    @pl.when(st == 0)
    def _():
        new = new_ref[0]
        pad = jnp.zeros((PAGE_SIZE - new.shape[0], SB_WIDTH), _F32)
        keys = _bf(jnp.concatenate([new[:, :SB_WIDTH], pad], axis=0))
        vals = _bf(jnp.concatenate([new[:, SB_WIDTH:], pad], axis=0))
        a, carry = _sb_chunk(_dot_nt(q, keys), jnp.zeros(car_scr.shape, _F32), tt, nmask_ref[...] > 0.5, True)
        acc_scr[...] = _dot(_bf(a), vals)
        car_scr[...] = carry

    keys_t = jnp.concatenate([_bf(r[0, 0]) for r in pages], axis=1)
    vals_t = jnp.concatenate([_bf(r[0, 1]) for r in pages], axis=1)
    a, carry = _sb_chunk(_dot(q, keys_t), car_scr[...], tt, None, True)
    acc_scr[...] = acc_scr[...] + _dot_nt(_bf(a), vals_t)
    car_scr[...] = carry

    @pl.when(st == pl.num_programs(1) - 1)
    def _():
        o_ref[0] = acc_scr[...]


def _sb_sample(page_flat, cache, q32, new8, nmask, tt):
    db, R, _ = q32.shape
    n_pages = page_flat.shape[0] // db
    pps = min(PAGES_PER_STEP, n_pages)
    n_st = n_pages // pps
    page_spec = lambda j: pl.BlockSpec(
        (1,) + cache.shape[1:], lambda b, s, pt: (pt[b * n_pages + n_pages - 1 - (s * pps + j)], 0, 0, 0))
    per_seq = lambda a: pl.BlockSpec((1,) + a.shape[1:], lambda b, s, pt: (b,) + (0,) * (a.ndim - 1))
    const = lambda a: pl.BlockSpec(a.shape, lambda b, s, pt: (0,) * a.ndim)
    return pl.pallas_call(
        _sb_sample_kernel,
        grid_spec=pltpu.PrefetchScalarGridSpec(
            num_scalar_prefetch=1,
            grid=(db, n_st),
            in_specs=[page_spec(j) for j in range(pps)] + [per_seq(q32), per_seq(new8), const(nmask), const(tt)],
            out_specs=pl.BlockSpec((1, R, SB_WIDTH), lambda b, s, pt: (b, 0, 0)),
            scratch_shapes=[pltpu.VMEM((R, SB_WIDTH), _F32), pltpu.VMEM((R, LANES), _F32)],
        ),
        out_shape=jax.ShapeDtypeStruct((db, R, SB_WIDTH), _F32),
        compiler_params=_params(("arbitrary", "arbitrary")),
        name="sb_sample",
    )(page_flat, *([cache] * pps), q32, new8, nmask, tt)


def _split_w_in(w_in):
    sizes = [NSA_WIDTH, 6 * KV_WIDTH, 3 * NSA_HEADS, SB_WIDTH, 2 * SB_WIDTH, 2 * D_MODEL]
    cuts = np.cumsum(sizes)[:-1].tolist()
    wqa, wkv, wgn, wqb, wsb, wgm = jnp.split(w_in, cuts, axis=1)
    wgn = jnp.pad(wgn, ((0, 0), (0, LANES - wgn.shape[1])))
    return [_bf(w) for w in (wqa, wkv, wgn, wqb, wsb, wgm)]


def _bucket_starts():
    b = _bucket_np(np.arange(4 * MAX_DISTANCE))
    assert b.max() == N_BUCKETS - 1 and (np.diff(b) >= 0).all()
    return [int(np.argmax(b >= k)) for k in range(1, N_BUCKETS)]


def _bias_of(dist, rb_ref, head, starts):
    out = jnp.full(dist.shape, rb_ref[N_BUCKETS - 1, head] * LOG2E, _F32)
    for b in range(N_BUCKETS - 2, -1, -1):
        out = jnp.where(dist < starts[b], rb_ref[b, head] * LOG2E, out)
    return out


def _bias_tables_kernel(rb_ref, bc_o, bs_o, bw_o, bfar_o, bfl_o, scmp_o, swin_o, slast_o, snew_o, sfar_o, *,
                        n_cmp_s):
    G, HPG, QT, Q = NSA_KV_GROUPS, HEADS_PER_GROUP, QUERY_TILE, DEC_SEQ
    starts = _bucket_starts()

    qi = lax.broadcasted_iota(jnp.int32, (1, QT), 1)

    def prompt_table(o_ref, dist_fn, valid_fn):
        r = lax.broadcasted_iota(jnp.int32, (o_ref.shape[1], 1), 0)
        dist = dist_fn(r, qi)
        valid = valid_fn(dist)
        for g in range(G):
            for h in range(HPG):
                o_ref[g, :, h * QT:(h + 1) * QT] = jnp.where(valid, _bias_of(dist, rb_ref, g * HPG + h, starts), NEG)

    prompt_table(bc_o, lambda r, i: i - CMP_STRIDE * (r - CMP_PAD) - (CMP_BLOCK - 1), lambda d: d >= 0)
    prompt_table(bs_o, lambda r, i: i + KEY_TILE - r, lambda d: d >= 0)
    prompt_table(bw_o, lambda r, i: i + WINDOW - r, lambda d: (d >= 0) & (d < WINDOW))
    for g in range(G):
        for h in range(HPG):
            bfar_o[g, :, h * QT:(h + 1) * QT] = jnp.full((1, QT), rb_ref[N_BUCKETS - 1, g * HPG + h] * LOG2E, _F32)
    nb = SEL_KEY_CHUNK // SEL_BLOCK
    lane = lax.broadcasted_iota(jnp.int32, (1, LANES), 1)
    bfl_o[...] = jnp.zeros(bfl_o.shape, _F32)
    for g in range(G):
        for h in range(HPG):
            far = jnp.full((1, LANES), rb_ref[N_BUCKETS - 1, g * HPG + h] * LOG2E, _F32)
            hi, lo = _split_hi_lo(far)
            bfl_o[g, h:h + 1, :] = jnp.where(lane == nb, hi.astype(_F32), jnp.where(lane == nb + 1, lo.astype(_F32), 0.0))

    GQ = G * Q
    row = lax.broadcasted_iota(jnp.int32, (GQ, 1), 0)
    row_g = row // Q
    t = PAST_LEN + row % Q

    def by_group(fn):
        out = fn(0)
        for g in range(1, G):
            out = jnp.where(row_g == g, fn(g), out)
        return out

    def sample_table(o_ref, kpos_fn, valid_fn):
        k = lax.broadcasted_iota(jnp.int32, (1, o_ref.shape[1]), 1)
        dist = t - kpos_fn(k)
        valid = valid_fn(dist, k)
        for h in range(HPG):
            b = by_group(lambda g: _bias_of(dist, rb_ref, g * HPG + h, starts))
            o_ref[h * GQ:(h + 1) * GQ, :] = jnp.where(valid, b, NEG)

    w_buf = swin_o.shape[1]
    sample_table(scmp_o, lambda k: CMP_STRIDE * k + CMP_BLOCK - 1, lambda d, k: (d >= 0) & (k < n_cmp_s))
    sample_table(swin_o, lambda k: PAST_LEN - w_buf + k,
                 lambda d, k: (d >= 0) & (d < WINDOW) & (PAST_LEN - w_buf + k >= 0))
    sample_table(slast_o, lambda k: PAST_LEN - PAGE_SIZE + k, lambda d, k: d >= 0)
    sample_table(snew_o, lambda k: PAST_LEN + k, lambda d, k: (d >= 0) & (k < Q))
    for h in range(HPG):
        far = by_group(lambda g: jnp.full((GQ, LANES), rb_ref[N_BUCKETS - 1, g * HPG + h] * LOG2E, _F32))
        sfar_o[h * GQ:(h + 1) * GQ, :] = far


def _bias_tables(rel_bias, n_chunk_s, n_cmp_s, w_buf):
    G, HPG, QT, Q = NSA_KV_GROUPS, HEADS_PER_GROUP, QUERY_TILE, DEC_SEQ
    NL, R = HPG * QT, HPG * G * Q
    f32 = lambda *s: jax.ShapeDtypeStruct(s, _F32)
    return pl.pallas_call(
        functools.partial(_bias_tables_kernel, n_cmp_s=n_cmp_s),
        in_specs=[pl.BlockSpec(memory_space=pltpu.SMEM)],
        out_shape=[f32(G, CMP_WIN, NL), f32(G, 2 * KEY_TILE, NL), f32(G, WINDOW + QT, NL), f32(G, 1, NL),
                   f32(G, SUBLANES, LANES), f32(R, n_chunk_s), f32(R, w_buf), f32(R, PAGE_SIZE), f32(R, PAGE_SIZE), f32(R, LANES)],
        compiler_params=_params(),
        name="bias_tables",
    )(rel_bias)


def kernel(x_prompt, x_sample, cache_cmp, cache_sel, cache_sb, state_win, state_conv, page_table,
           rel_bias, g_pre_mix, w_in, w_ck1, b_ck1, w_ck2, w_cv1, b_cv1, w_cv2, w_branch_a, w_branch_b,
           w_out, g_post_mix, g_pre_ffn, w_ffn_up, conv_w, conv_b, w_ffn_down, g_post_ffn):
    assert DEPTH == 1 and x_prompt.shape[0] == 1
    G, HPG, dh, Q = NSA_KV_GROUPS, HEADS_PER_GROUP, HEAD_DIM, DEC_SEQ
    T = x_prompt.shape[1]
    DB = x_sample.shape[0]
    n_pages = page_table.shape[1]
    w_buf = state_win.shape[2]
    row = lambda v: v.reshape(1, -1)

    w_proj = _split_w_in(w_in[0])
    g_pre = row(g_pre_mix[0])
    w1k, w1v = _bf(_chunk_weights(w_ck1[0])), _bf(_chunk_weights(w_cv1[0]))
    b1k, b1v = row(jnp.tile(b_ck1[0], G)), row(jnp.tile(b_cv1[0], G))
    w2k, w2v = _bf(_group_diag(w_ck2[0])), _bf(_group_diag(w_cv2[0]))
    wa, wb, wo = _bf(w_branch_a[0]), _bf(w_branch_b[0]), _bf(w_out[0])
    wua, wub = _bf(w_ffn_up[0][:, :D_FF]), _bf(w_ffn_up[0][:, D_FF:])
    wd = _bf(w_ffn_down[0])
    cw, cb = conv_w[0], row(conv_b[0])
    g_post_m, g_pre_f, g_post_f = row(g_post_mix[0]), row(g_pre_ffn[0]), row(g_post_ffn[0])
    tt = jnp.asarray(_cumsum_matrix(), _BF16)

    xp = x_prompt[0]
    qa, kv, kvb, gn, qb, sb, sbb, gm = _proj_in(xp, g_pre, w_proj)
    n_chunk = T // CMP_STRIDE
    n_cmp = (T - CMP_BLOCK) // CMP_STRIDE + 1
    kvcol = lambda a, j: a[:, j * KV_WIDTH:(j + 1) * KV_WIDTH]
    kc, vc = _compress_prompt(kvcol(kvb, 0).reshape(n_chunk, CMP_STRIDE * KV_WIDTH),
                              kvcol(kvb, 1).reshape(n_chunk, CMP_STRIDE * KV_WIDTH),
                              w1k, w1v, b1k, b1v, w2k, w2v)
    kcp = jnp.pad(kc, ((CMP_PAD, 0), (0, 0)))
    vcp = jnp.pad(vc, ((CMP_PAD, 0), (0, 0)))
    front = lambda a: jnp.pad(a, ((WINDOW, 0), (0, 0)))
    cpp = PAGE_SIZE // CMP_STRIDE
    n_chunk_s = n_pages * cpp
    n_cmp_s = (PAST_LEN + Q - CMP_BLOCK) // CMP_STRIDE + 1
    bc, bs, bw, bfar, bfl, bcmp, bwin, blast, bnew, bfar_s = _bias_tables(rel_bias, n_chunk_s, n_cmp_s, w_buf)
    ct =jnp.asarray(_cover_matrix(n_chunk + CMP_PAD, T // SEL_BLOCK, n_cmp, CMP_PAD).T, _BF16)
    o_a = _nsa_prompt(qa, gn, kcp, vcp, front(kvcol(kvb, 2)), front(kvcol(kvb, 3)),
                      front(kvcol(kvb, 4)), front(kvcol(kvb, 5)), bc, bs, bw, bfar, bfl, ct)
    o_b = _sb_prompt(qb, sbb, tt)
    x1, h2 = _merge(xp, o_a, o_b, gm, wa, wb, wo, g_post_m, g_pre_f)
    y_p, tails = _ffn_prompt(h2, x1, wua, wub, cw, cb, wd, g_post_f)

    kv_cmp_p = kv[:, 0:2 * KV_WIDTH].reshape(1, 1, T, 2, G, dh)
    kv_sel_p = kv[:, 2 * KV_WIDTH:4 * KV_WIDTH].reshape(1, 1, T, 2, G, dh)
    kv_sb_p = sb.reshape(1, 1, T, 2, SB_HEADS, dh)
    win_rows = jnp.pad(kv[:, 4 * KV_WIDTH:], ((WINDOW, 0), (0, 0)))[T + WINDOW - w_buf:]
    win_p = win_rows.reshape(1, 1, w_buf, 2, G, dh)
    conv_p = tails[-(CONV_W - 1):].reshape(1, 1, CONV_W - 1, D_FF)

    xs = jnp.transpose(x_sample, (1, 0, 2)).reshape(Q * DB, D_MODEL)
    qa_s, kv_s, _, gn_s, qb_s, sb_s, _, gm_s = _proj_in(xs, g_pre, w_proj)
    by_seq = lambda a: jnp.transpose(a.reshape(Q, DB, -1), (1, 0, 2))
    page_flat = page_table.reshape(-1)

    assert n_cmp_s < n_chunk_s + 1 and Q <= SUBLANES and Q <= CMP_STRIDE
    parts = _cmp_pages(page_flat, _pages_last(cache_cmp[0]), _bf(_page_chunk_weights(w_ck1[0], w_cv1[0])))
    parts = parts.reshape(DB, n_chunk_s, 4 * KV_WIDTH)

    q5 = by_seq(qa_s).reshape(DB, Q, G, HPG, dh)
    q5 = jnp.transpose(q5, (0, 3, 2, 1, 4))
    q32 = jnp.einsum('bhgqd,gk->bhgqkd', q5, jnp.eye(G, dtype=q5.dtype)).reshape(DB, HPG * G * Q, KV_WIDTH)
    g5 = by_seq(gn_s)[:, :, :3 * NSA_HEADS].reshape(DB, Q, 3, G, HPG)
    gate32 = jnp.transpose(g5, (0, 4, 3, 1, 2)).reshape(DB, HPG * G * Q, 3)
    gate32 = jnp.pad(gate32, ((0, 0), (0, 0), (0, LANES - 3)))
    new8 = jnp.pad(by_seq(kv_s), ((0, 0), (0, SUBLANES - Q), (0, 0)))
    n_sel_s = -(-(PAST_LEN + Q) // SEL_BLOCK)
    n_blk_s = -(-n_sel_s // LANES) * LANES
    cmat =jnp.asarray(_cover_matrix(n_chunk_s, n_blk_s, n_cmp_s, 0), _BF16)
    cmat = (cmat * (np.arange(n_blk_s)[None, :] < n_sel_s)).T
    cur_blocks = tuple(int((PAST_LEN + j) // SEL_BLOCK) for j in range(Q))
    o32 = _nsa_sample(page_flat, _pages_last(cache_sel[0]), q32, gate32, parts,
                      _pages_last(state_win[0]), new8, b1k, b1v, w2k, w2v,
                      bcmp, bwin, blast, bnew, bfar_s, cmat, n_cmp_s, n_sel_s, cur_blocks)
    o6 = o32.reshape(DB, HPG, G, Q, G, dh)
    o_a_s = jnp.stack([o6[:, :, g, :, g, :] for g in range(G)], axis=1)
    o_a_s = jnp.transpose(o_a_s, (3, 0, 1, 2, 4)).reshape(Q * DB, NSA_WIDTH)

    qb5 = jnp.transpose(by_seq(qb_s).reshape(DB, Q, SB_HEADS, dh), (0, 2, 1, 3))
    qsb = jnp.einsum('bhqd,hk->bhqkd', qb5, jnp.eye(SB_HEADS, dtype=qb5.dtype)).reshape(DB, SB_HEADS * Q, SB_WIDTH)
    sb_new8 = jnp.pad(by_seq(sb_s), ((0, 0), (0, SUBLANES - Q), (0, 0)))
    rq = np.arange(SB_HEADS * Q)[:, None] % Q
    nmask = jnp.asarray((np.arange(PAGE_SIZE)[None, :] < rq).astype(np.float32))
    osb = _sb_sample(page_flat, _pages_last(cache_sb[0]), qsb, sb_new8, nmask, tt)
    o7 = osb.reshape(DB, SB_HEADS, Q, SB_HEADS, dh)
    o_b_s = jnp.stack([o7[:, h, :, h, :] for h in range(SB_HEADS)], axis=1)
    o_b_s = jnp.transpose(o_b_s, (2, 0, 1, 3)).reshape(Q * DB, SB_WIDTH)

    x1_s, h2_s = _merge(xs, _bf(o_a_s), _bf(o_b_s), gm_s, wa, wb, wo, g_post_m, g_pre_f)
    sc = jnp.transpose(state_conv[0], (1, 0, 2))
    y_s_rows, tail_s = _ffn_sample(h2_s, x1_s, sc, wua, wub, cw, cb, wd, g_post_f)

    y_s = by_seq(y_s_rows)
    kv_seq = by_seq(kv_s)
    kv_cmp_s = kv_seq[:, :, 0:2 * KV_WIDTH].reshape(1, DB, Q, 2, G, dh)
    kv_sel_s = kv_seq[:, :, 2 * KV_WIDTH:4 * KV_WIDTH].reshape(1, DB, Q, 2, G, dh)
    kv_sb_s = by_seq(sb_s).reshape(1, DB, Q, 2, SB_HEADS, dh)
    win_new = kv_seq[:, :, 4 * KV_WIDTH:].reshape(DB, Q, 2, G, dh)
    win_s = jnp.concatenate([state_win[0], win_new], axis=1)[:, -w_buf:][None]
    conv_s = jnp.transpose(tail_s.reshape(CONV_W - 1, DB, D_FF), (1, 0, 2))[None]

    return (y_p[None], y_s, kv_cmp_p, kv_sel_p, kv_sb_p, win_p, conv_p,
            kv_cmp_s, kv_sel_s, kv_sb_s, win_s, conv_s)
```

```python
import functools
import math

import numpy as np
import jax
import jax.numpy as jnp
from jax import lax
from jax.experimental import pallas as pl
from jax.experimental.pallas import tpu as pltpu

D_MODEL = 1024
SEQ = 16384
DEPTH = 1
DEC_BATCH = 128
DEC_SEQ = 4
PAST_LEN = 8192
PAGE_SIZE = 128
HEAD_DIM = 64
NSA_HEADS = 8
NSA_KV_GROUPS = 2
HEADS_PER_GROUP = NSA_HEADS // NSA_KV_GROUPS
SB_HEADS = 8
NSA_WIDTH = NSA_HEADS * HEAD_DIM
SB_WIDTH = SB_HEADS * HEAD_DIM
KV_WIDTH = NSA_KV_GROUPS * HEAD_DIM
CMP_BLOCK = 32
CMP_STRIDE = 16
SEL_BLOCK = 64
SEL_TOP = 16
SEL_COVER_W = (0.5, 1.0, 1.0, 1.0, 0.5)
WINDOW = 512
N_BUCKETS = 32
MAX_DISTANCE = 128
D_FF = 2816
CONV_W = 3
SCALE = HEAD_DIM ** -0.5
LOG2E = math.log2(math.e)
EPS = 1e-6
NEG = -1e30
FORCE_SCORE = 1e9
MASKED_BELOW = -5e29
MASK_PENALTY = -NEG
LANES = 128
SUBLANES = 8
QUERY_TILE = 128
KEY_TILE = 128
ROW_TILE = 256
CMP_PAD = 16
CMP_WIN = 24
PAGES_PER_STEP = 16
SB_QUERY_TILE = 256
SB_KEY_CHUNK = 1024
SEL_KEY_CHUNK = 512
VMEM_LIMIT = 56 * 1024 * 1024

_F32 = jnp.float32
_BF16 = jnp.bfloat16


def _bf(x):
    return x.astype(_BF16)


def _dot(a, b):
    return jnp.dot(a, b, preferred_element_type=_F32)


def _dot_nt(a, b):
    return lax.dot_general(a, b, (((1,), (1,)), ((), ())), preferred_element_type=_F32)


def _dot_tn(a, b):
    return lax.dot_general(a, b, (((0,), (0,)), ((), ())), preferred_element_type=_F32)


def _split_hi_lo(x):
    hi = _bf(x)
    lo = _bf(x - hi.astype(_F32))
    return hi, lo


def _rms(x, g):
    return x * lax.rsqrt(jnp.mean(x * x, axis=-1, keepdims=True) + EPS) * g


def _softplus_log2(z2):
    return jnp.maximum(z2, 0.0) + jnp.log2(1.0 + jnp.exp2(-jnp.abs(z2)))


def _const_spec(shape):
    nd = len(shape)
    return pl.BlockSpec(shape, lambda *_: (0,) * nd)


def _params(sem=None):
    return pltpu.CompilerParams(dimension_semantics=sem, vmem_limit_bytes=VMEM_LIMIT)


def _bucket_np(dist):
    n = np.maximum(dist, 0)
    max_exact = N_BUCKETS // 2
    nf = np.maximum(n, 1).astype(np.float32)
    large = max_exact + (np.log(nf / max_exact) / math.log(MAX_DISTANCE / max_exact)
                         * (N_BUCKETS - max_exact)).astype(np.int32)
    return np.where(n < max_exact, n, np.minimum(large, N_BUCKETS - 1)).astype(np.int32)


def _cover_matrix(n_rows, n_cols, n_cmp, row_offset):
    r = np.arange(n_rows)[:, None] - row_offset
    j = np.arange(n_cols)[None, :]
    k = r - 4 * j + 1
    w = np.asarray(SEL_COVER_W, np.float32)
    ok = (k >= 0) & (k <= 4) & (r >= 0) & (r < n_cmp)
    return np.where(ok, w[np.clip(k, 0, 4)], 0.0).astype(np.float32)


def _cumsum_matrix(latest_first):
    k = np.arange(2 * LANES)
    same = (k[:, None] // LANES) == (k[None, :] // LANES)
    within = same & (k[:, None] >= k[None, :])
    if latest_first:
        cross = (k[:, None] < LANES) & (k[None, :] >= LANES)
    else:
        cross = (k[:, None] >= LANES) & (k[None, :] < LANES)
    return (within | cross).astype(np.float32)


def _proj_in_kernel(x_ref, g_ref, wqa, wkv, wgn, wqb, wsb, wgm,
                    qa_o, kv_o, kvb_o, gn_o, qb_o, sb_o, sbb_o, gm_o):
    h = _bf(_rms(x_ref[...], g_ref[...]))
    qa_o[...] = _bf(_dot(h, wqa[...]) * (SCALE * LOG2E))
    kv = _dot(h, wkv[...])
    kv_o[...] = kv
    kvb_o[...] = _bf(kv)
    gn_o[...] = _dot(h, wgn[...])
    qb_o[...] = _bf(_dot(h, wqb[...]) * (SCALE * LOG2E))
    sb = _dot(h, wsb[...])
    sb_o[...] = sb
    sbb_o[...] = _bf(sb)
    gm_o[...] = _dot(h, wgm[...])


def _proj_in(x, g, ws):
    rows = x.shape[0]
    tm = min(ROW_TILE, rows)
    widths = [w.shape[1] for w in ws]
    out_shape = [
        jax.ShapeDtypeStruct((rows, widths[0]), _BF16),
        jax.ShapeDtypeStruct((rows, widths[1]), _F32),
        jax.ShapeDtypeStruct((rows, widths[1]), _BF16),
        jax.ShapeDtypeStruct((rows, widths[2]), _F32),
        jax.ShapeDtypeStruct((rows, widths[3]), _BF16),
        jax.ShapeDtypeStruct((rows, widths[4]), _F32),
        jax.ShapeDtypeStruct((rows, widths[4]), _BF16),
        jax.ShapeDtypeStruct((rows, widths[5]), _F32),
    ]
    row_spec = lambda w: pl.BlockSpec((tm, w), lambda i: (i, 0))
    return pl.pallas_call(
        _proj_in_kernel,
        grid=(rows // tm,),
        in_specs=[row_spec(D_MODEL), _const_spec((1, D_MODEL))] + [_const_spec(w.shape) for w in ws],
        out_specs=[row_spec(s.shape[1]) for s in out_shape],
        out_shape=out_shape,
        compiler_params=_params(("arbitrary",)),
        name="proj_in",
    )(x, g, *ws)


def _compress_finish(parts_lo, parts_hi, b1, w2):
    n = parts_hi.shape[0]
    hi_next = pltpu.roll(parts_hi, n - 1, 0)
    hid = parts_lo + hi_next + b1
    return _dot(_bf(jax.nn.gelu(hid)), w2)


def _compress_kernel(ck_ref, cv_ref, w1k, w1v, b1k, b1v, w2k, w2v, kc_o, vc_o):
    for c_ref, w1, b1, w2, o in ((ck_ref, w1k, b1k, w2k, kc_o), (cv_ref, w1v, b1v, w2v, vc_o)):
        parts = _dot(c_ref[...], w1[...])
        o[...] = _bf(_compress_finish(parts[:, :KV_WIDTH], parts[:, KV_WIDTH:], b1[...], w2[...]))


def _compress_prompt(ck, cv, w1k, w1v, b1k, b1v, w2k, w2v):
    n = ck.shape[0]
    return pl.pallas_call(
        _compress_kernel,
        out_shape=[jax.ShapeDtypeStruct((n, KV_WIDTH), _BF16)] * 2,
        compiler_params=_params(),
        name="compress_prompt",
    )(ck, cv, w1k, w1v, b1k, b1v, w2k, w2v)


def _chunk_weights(w1):
    G, dh = NSA_KV_GROUPS, HEAD_DIM
    w1r = w1.reshape(2, CMP_STRIDE, dh, dh)
    eye = jnp.eye(G, dtype=w1.dtype)
    big = jnp.einsum('slde,gh->lgdshe', w1r, eye)
    return big.reshape(CMP_STRIDE * G * dh, 2 * G * dh)


def _group_diag(w2):
    return jnp.kron(jnp.eye(NSA_KV_GROUPS, dtype=w2.dtype), w2)


def _nsa_prompt_kernel(qa_ref, gn_ref, kcp_ref, vcp_ref, ks_ref, vs_ref, kw_ref, vw_ref,
                       bc_ref, bs_ref, bw_ref, bfar_ref, bfl_ref, ct_ref, pat_ref,
                       oa_ref, s_scr, sel_scr, q_scr, selm_scr, sa_scr, sb_scr):
    G, HPG, dh = NSA_KV_GROUPS, HEADS_PER_GROUP, HEAD_DIM
    QT = QUERY_TILE
    NL = HPG * QT
    qb = pl.program_id(0)
    c = qb * QT
    n_cp = kcp_ref.shape[0]
    n_blk = ct_ref.shape[0]

    lane_half = lax.broadcasted_iota(jnp.int32, (1, LANES), 1) // dh
    gates_t = jnp.transpose(jax.nn.sigmoid(gn_ref[...]))

    def gate_row(branch, g):
        base = branch * NSA_HEADS + g * HPG
        return jnp.concatenate([gates_t[base + h:base + h + 1, :] for h in range(HPG)], axis=1)

    ti = c + lax.broadcasted_iota(jnp.int32, (1, QT), 1)
    cur = ti // SEL_BLOCK

    for g in range(G):
        q_rows = []
        for h in range(HPG):
            col = (g * HPG + h) * dh
            tile = qa_ref[:, (col // LANES) * LANES:(col // LANES + 1) * LANES].astype(_F32)
            if (col % LANES) // dh != g:
                tile = pltpu.roll(tile, dh, 1)
            q_rows.append(_bf(jnp.where(lane_half == g, tile, 0.0)))
        q_scr[g] = jnp.concatenate(q_rows, axis=0)

    o_cmp = [None] * G
    for g in range(G):
        qg = q_scr[g]
        bfar = bfar_ref[g]

        r0 = pl.multiple_of(qb * (QT // CMP_STRIDE), SUBLANES)
        s_scr[...] = _dot_nt(kcp_ref[...], qg) + bfar
        s_scr[pl.ds(r0, CMP_WIN), :] = s_scr[pl.ds(r0, CMP_WIN), :] - bfar + bc_ref[g]
        row = lax.broadcasted_iota(jnp.int32, (n_cp, 1), 0)
        s = jnp.where((row >= CMP_PAD) & (row < r0 + CMP_WIN), s_scr[...], NEG)
        m = jnp.max(s, axis=0, keepdims=True)
        e = jnp.exp2(s - m)
        p = e * jnp.where(m > MASKED_BELOW, 1.0 / jnp.sum(e, axis=0, keepdims=True), 0.0)
        o_cmp[g] = _dot_tn(vcp_ref[...], _bf(p))[g * dh:(g + 1) * dh]
        p_grp = p[:, 0:QT]
        for h in range(1, HPG):
            p_grp = p_grp + p[:, h * QT:(h + 1) * QT]
        p_hi, p_lo = _split_hi_lo(p_grp)
        p_slc = _dot(ct_ref[...], p_hi) + _dot(ct_ref[...], p_lo)

        blk = lax.broadcasted_iota(jnp.int32, (n_blk, 1), 0)
        blk_f = blk.astype(_F32)
        forced = (blk == 0) | (blk == cur) | (blk == cur - 1)
        score = jnp.where(forced, FORCE_SCORE, jnp.where(blk <= cur, p_slc, NEG))
        for _ in range(min(SEL_TOP, n_blk)):
            best = jnp.max(score, axis=0, keepdims=True)
            first = jnp.min(jnp.where(score == best, blk_f, float(n_blk)), axis=0, keepdims=True)
            score = jnp.where(blk_f == first, -jnp.inf, score)
        sel = jnp.where(score == -jnp.inf, 1.0, 0.0)
        sel_scr[g] = jnp.concatenate([sel] * HPG, axis=1)
        selm_scr[g] = _bf((jnp.transpose(sel) - 1.0) * MASK_PENALTY)

    def sel_rows(g, first_blk, n):
        rows = [jnp.broadcast_to(sel_scr[g, pl.ds(first_blk + j, 1), :], (SEL_BLOCK, NL)) for j in range(n)]
        return jnp.concatenate(rows, axis=0) > 0.5

    def sel_step(g, vals, s_tile, mask, carry, zero_masked):
        m_run, l_run, acc = carry
        if mask is not None:
            s_tile = jnp.where(mask, s_tile, NEG)
        m_new = jnp.maximum(m_run, jnp.max(s_tile, axis=0, keepdims=True))
        alpha = jnp.exp2(m_run - m_new)
        p_t = jnp.exp2(s_tile - m_new)
        if zero_masked:
            p_t = jnp.where(mask, p_t, 0.0)
        l_new = alpha * l_run + jnp.sum(p_t, axis=0, keepdims=True)
        pv = _dot_tn(vals, _bf(p_t))[g * dh:(g + 1) * dh]
        return m_new, l_new, alpha * acc + pv

    n_far_blk = jnp.maximum(qb - 1, 0) * (KEY_TILE // SEL_BLOCK)
    blk_per_chunk = SEL_KEY_CHUNK // SEL_BLOCK
    blk_col = lax.broadcasted_iota(jnp.int32, (n_blk, 1), 0)
    lane = lax.broadcasted_iota(jnp.int32, (1, LANES), 1)

    n_far_chunks = (n_far_blk + blk_per_chunk - 1) // blk_per_chunk

    def chunk_rows(f):
        return pl.multiple_of(f * SEL_KEY_CHUNK + WINDOW, KEY_TILE)

    def far_scores(f, out_scr):
        rows = chunk_rows(jnp.minimum(f, jnp.maximum(n_far_chunks - 1, 0)))
        keys = jnp.concatenate([ks_ref[pl.ds(rows, SEL_KEY_CHUNK), :], pat_ref[...]], axis=1)
        blk_of_lane = f * blk_per_chunk + lane
        pick = _bf(jnp.where((blk_col == blk_of_lane) & (lane < blk_per_chunk), 1.0, 0.0))
        off = jnp.where((lane < blk_per_chunk) & (blk_of_lane >= n_far_blk), -MASK_PENALTY, 0.0)
        pens = [_dot(selm_scr[g], pick) + off for g in range(G)]
        for g in range(G):
            extra = jnp.concatenate([_bf(pens[g] + bfl_ref[g, h:h + 1, :]) for h in range(HPG)], axis=0)
            q_aug = jnp.concatenate([q_scr[g], extra], axis=1)
            out_scr[g] = _dot_nt(keys, q_aug)

    def far_consume(f, in_scr, state):
        vals = vs_ref[pl.ds(chunk_rows(jnp.minimum(f, jnp.maximum(n_far_chunks - 1, 0))), SEL_KEY_CHUNK), :]
        return tuple(sel_step(g, vals, in_scr[g], None, state[g], False) for g in range(G))

    def far_body(k, state):
        far_scores(2 * k + 1, sb_scr)
        state = far_consume(2 * k, sa_scr, state)
        far_scores(2 * k + 2, sa_scr)
        return far_consume(2 * k + 1, sb_scr, state)

    init = tuple((jnp.full((1, NL), NEG, _F32), jnp.zeros((1, NL), _F32), jnp.zeros((dh, NL), _F32))
                 for _ in range(G))
    far_scores(0, sa_scr)
    far = lax.fori_loop(0, (n_far_chunks + 1) // 2, far_body, init)

    for g in range(G):
        qg = q_scr[g]
        rn = pl.multiple_of(c + WINDOW - KEY_TILE, KEY_TILE)
        keys = ks_ref[pl.ds(rn, 2 * KEY_TILE), :]
        vals = vs_ref[pl.ds(rn, 2 * KEY_TILE), :]
        bias_near = bs_ref[g]
        s_tile = _dot_nt(keys, qg) + bias_near
        kpos = c - KEY_TILE + lax.broadcasted_iota(jnp.int32, (2 * KEY_TILE, 1), 0)
        first_blk = jnp.maximum(qb - 1, 0) * (KEY_TILE // SEL_BLOCK)
        near_sel = jnp.concatenate([
            jnp.where(qb > 0, 1.0, 0.0) * sel_rows(g, first_blk, KEY_TILE // SEL_BLOCK).astype(_F32),
            sel_rows(g, qb * (KEY_TILE // SEL_BLOCK), KEY_TILE // SEL_BLOCK).astype(_F32)], axis=0) > 0.5
        mask = near_sel & (bias_near > MASKED_BELOW) & (kpos >= 0)
        _, l_fin, acc = sel_step(g, vals, s_tile, mask, far[g], True)
        o_sel = acc * (1.0 / jnp.where(l_fin > 0.0, l_fin, 1.0))

        rw = pl.multiple_of(c, KEY_TILE)
        n_w = WINDOW + QT
        bias_w = bw_ref[g]
        s_w = _dot_nt(kw_ref[pl.ds(rw, n_w), :], qg) + bias_w
        kpos_w = c - WINDOW + lax.broadcasted_iota(jnp.int32, (n_w, 1), 0)
        mask_w = (bias_w > MASKED_BELOW) & (kpos_w >= 0)
        s_w = jnp.where(mask_w, s_w, NEG)
        m_w = jnp.max(s_w, axis=0, keepdims=True)
        e_w = jnp.exp2(s_w - m_w)
        p_w = e_w * jnp.where(m_w > MASKED_BELOW, 1.0 / jnp.sum(e_w, axis=0, keepdims=True), 0.0)
        o_win = _dot_tn(vw_ref[pl.ds(rw, n_w), :], _bf(p_w))[g * dh:(g + 1) * dh]

        o_t = gate_row(0, g) * o_cmp[g] + gate_row(1, g) * o_sel + gate_row(2, g) * o_win
        for pair in range(HPG // 2):
            blk_t = jnp.concatenate([o_t[:, (2 * pair) * QT:(2 * pair + 1) * QT],
                                     o_t[:, (2 * pair + 1) * QT:(2 * pair + 2) * QT]], axis=0)
            col = (g * HPG + 2 * pair) * dh
            oa_ref[:, col:col + LANES] = _bf(jnp.transpose(blk_t))


def _chunk_pattern():
    nb = SEL_KEY_CHUNK // SEL_BLOCK
    pat = np.zeros((SEL_KEY_CHUNK, LANES), np.float32)
    pat[np.arange(SEL_KEY_CHUNK), np.arange(SEL_KEY_CHUNK) // SEL_BLOCK] = 1.0
    pat[:, nb:nb + 2] = 1.0
    return pat


def _nsa_prompt(qa, gn, kcp, vcp, ks, vs, kw, vw, bc, bs, bw, bfar, bfl, ct):
    T = qa.shape[0]
    QT = QUERY_TILE
    NL = HEADS_PER_GROUP * QT
    pat = jnp.asarray(_chunk_pattern(), _BF16)
    full = lambda a: _const_spec(a.shape)
    return pl.pallas_call(
        _nsa_prompt_kernel,
        grid=(T // QT,),
        in_specs=[pl.BlockSpec((QT, NSA_WIDTH), lambda i: (i, 0)),
                  pl.BlockSpec((QT, LANES), lambda i: (i, 0)),
                  full(kcp), full(vcp), full(ks), full(vs), full(kw), full(vw),
                  full(bc), full(bs), full(bw), full(bfar), full(bfl), full(ct), full(pat)],
        out_specs=pl.BlockSpec((QT, NSA_WIDTH), lambda i: (i, 0)),
        out_shape=jax.ShapeDtypeStruct((T, NSA_WIDTH), _BF16),
        scratch_shapes=[pltpu.VMEM((kcp.shape[0], NL), _F32),
                        pltpu.VMEM((NSA_KV_GROUPS, ct.shape[0], NL), _F32),
                        pltpu.VMEM((NSA_KV_GROUPS, NL, LANES), _BF16),
                        pltpu.VMEM((NSA_KV_GROUPS, QT, ct.shape[0]), _BF16),
                        pltpu.VMEM((NSA_KV_GROUPS, SEL_KEY_CHUNK, NL), _F32),
                        pltpu.VMEM((NSA_KV_GROUPS, SEL_KEY_CHUNK, NL), _F32)],
        compiler_params=_params(("arbitrary",)),
        name="nsa_prompt",
    )(qa, gn, kcp, vcp, ks, vs, kw, vw, bc, bs, bw, bfar, bfl, ct, pat)


def _sb_chunks(zs, carries, tt, mask, latest_first):
    width = zs[0].shape[1]
    step = min(2 * LANES, width)
    tri = tt if step == 2 * LANES else tt[:LANES, :LANES]
    n = width // step
    sps = []
    for z in zs:
        sp = _softplus_log2(z)
        sps.append(sp if mask is None else jnp.where(mask, sp, 0.0))
    carries = list(carries)
    a_parts = [[None] * n for _ in zs]
    for j in (range(n) if latest_first else reversed(range(n))):
        sl = slice(j * step, (j + 1) * step)
        css = [_dot(_bf(sp[:, sl]), tri) for sp in sps]
        for i, z in enumerate(zs):
            later = jnp.concatenate([carries[i]] * (step // LANES), axis=1)
            a_parts[i][j] = jnp.exp2(z[:, sl] - css[i] - later)
            carries[i] = carries[i] + jnp.sum(sps[i][:, sl], axis=1, keepdims=True)
    out = []
    for pieces in a_parts:
        a = pieces[0] if n == 1 else jnp.concatenate(pieces, axis=1)
        out.append(a if mask is None else jnp.where(mask, a, 0.0))
    return out, carries


def _sb_prompt_kernel(q_ref, k_ref, v_ref, tt_ref, o_ref, acc_scr, car_scr):
    dh = HEAD_DIM
    QT = q_ref.shape[0]
    KW = min(SB_KEY_CHUNK, k_ref.shape[0])
    i = pl.program_id(1)
    lane_half = lax.broadcasted_iota(jnp.int32, (1, LANES), 1) // dh
    q = q_ref[...].astype(_F32)
    qh = [_bf(jnp.where(lane_half == h, q, 0.0)) for h in range(2)]
    tt = tt_ref[...]
    last = (i * QT + QT - 1) // KW

    def chunk(ci, mask):
        r = pl.multiple_of(ci * KW, KW)
        keys = k_ref[pl.ds(r, KW), :]
        vals = v_ref[pl.ds(r, KW), :]
        zs = [_dot_nt(qh[h], keys) for h in range(2)]
        ws, carries = _sb_chunks(zs, [car_scr[h] for h in range(2)], tt, mask, latest_first=False)
        for h in range(2):
            acc_scr[h] = acc_scr[h] + _dot(_bf(ws[h]), vals)
            car_scr[h] = carries[h]

    acc_scr[...] = jnp.zeros(acc_scr.shape, _F32)
    car_scr[...] = jnp.zeros(car_scr.shape, _F32)
    t = i * QT + lax.broadcasted_iota(jnp.int32, (QT, 1), 0)
    kpos = last * KW + lax.broadcasted_iota(jnp.int32, (1, KW), 1)
    chunk(last, kpos < t)

    def body(n, _):
        chunk(last - 1 - n, None)
        return 0

    lax.fori_loop(0, last, body, 0)
    o_ref[...] = _bf(jnp.where(lane_half == 0, acc_scr[0], acc_scr[1]))


def _sb_prompt(qb, sbb, tt):
    T = qb.shape[0]
    QT = min(SB_QUERY_TILE, T)
    n_pair = SB_WIDTH // LANES
    return pl.pallas_call(
        _sb_prompt_kernel,
        grid=(n_pair, T // QT),
        in_specs=[pl.BlockSpec((QT, LANES), lambda p, i: (i, p)),
                  pl.BlockSpec((T, LANES), lambda p, i: (0, p)),
                  pl.BlockSpec((T, LANES), lambda p, i: (0, n_pair + p)),
                  _const_spec(tt.shape)],
        out_specs=pl.BlockSpec((QT, LANES), lambda p, i: (i, p)),
        out_shape=jax.ShapeDtypeStruct((T, SB_WIDTH), _BF16),
        scratch_shapes=[pltpu.VMEM((2, QT, LANES), _F32), pltpu.VMEM((2, QT, LANES), _F32)],
        compiler_params=_params(("arbitrary", "arbitrary")),
        name="sb_prompt",
    )(qb, sbb, sbb, tt)


def _merge_kernel(x_ref, oa_ref, ob_ref, gm_ref, wa, wb, wo, g_post, g_pre, x1_o, h2_o):
    gm = gm_ref[...]
    mixed = (jax.nn.sigmoid(gm[:, :D_MODEL]) * _dot(oa_ref[...], wa[...])
             + jax.nn.sigmoid(gm[:, D_MODEL:]) * _dot(ob_ref[...], wb[...]))
    x1 = x_ref[...] + _rms(_dot(_bf(mixed), wo[...]), g_post[...])
    x1_o[...] = x1
    h2_o[...] = _bf(_rms(x1, g_pre[...]))


def _merge(x, oa, ob, gm, wa, wb, wo, g_post, g_pre):
    rows = x.shape[0]
    tm = min(ROW_TILE, rows)
    row_spec = lambda w: pl.BlockSpec((tm, w), lambda i: (i, 0))
    return pl.pallas_call(
        _merge_kernel,
        grid=(rows // tm,),
        in_specs=[row_spec(D_MODEL), row_spec(NSA_WIDTH), row_spec(SB_WIDTH), row_spec(2 * D_MODEL),
                  _const_spec(wa.shape), _const_spec(wb.shape), _const_spec(wo.shape),
                  _const_spec((1, D_MODEL)), _const_spec((1, D_MODEL))],
        out_specs=[row_spec(D_MODEL), row_spec(D_MODEL)],
        out_shape=[jax.ShapeDtypeStruct((rows, D_MODEL), _F32), jax.ShapeDtypeStruct((rows, D_MODEL), _BF16)],
        compiler_params=_params(("arbitrary",)),
        name="merge",
    )(x, oa, ob, gm, wa, wb, wo, g_post, g_pre)


def _ffn_tail(x1, conv, gate, wd, g_post):
    f = jax.nn.gelu(conv) * gate
    return x1 + _rms(_dot(_bf(f), wd), g_post)


def _ffn_prompt_kernel(h2_ref, halo_ref, x1_ref, wua, wub, cw_ref, cb_ref, wd, g_post, y_o, tail_o):
    i = pl.program_id(0)
    tm = h2_ref.shape[0]
    h2 = h2_ref[...]
    a_ext = _dot(jnp.concatenate([halo_ref[...], h2], axis=0), wua[...])
    row = lax.broadcasted_iota(jnp.int32, (SUBLANES + tm, 1), 0)
    a_ext = jnp.where((row < SUBLANES) & (i == 0), 0.0, a_ext)
    a0 = a_ext[SUBLANES:]
    a1 = pltpu.roll(a_ext, 1, 0)[SUBLANES:]
    a2 = pltpu.roll(a_ext, 2, 0)[SUBLANES:]
    cw = cw_ref[...]
    conv = cb_ref[...] + cw[0:1] * a2 + cw[1:2] * a1 + cw[2:3] * a0
    y_o[...] = _ffn_tail(x1_ref[...], conv, _dot(h2, wub[...]), wd[...], g_post[...])
    tail_o[...] = a0[tm - SUBLANES:]


def _ffn_prompt(h2, x1, wua, wub, cw, cb, wd, g_post):
    T = h2.shape[0]
    tm = min(ROW_TILE, T)
    hb = tm // SUBLANES
    return pl.pallas_call(
        _ffn_prompt_kernel,
        grid=(T // tm,),
        in_specs=[pl.BlockSpec((tm, D_MODEL), lambda i: (i, 0)),
                  pl.BlockSpec((SUBLANES, D_MODEL), lambda i: (jnp.maximum(i * hb - 1, 0), 0)),
                  pl.BlockSpec((tm, D_MODEL), lambda i: (i, 0)),
                  _const_spec(wua.shape), _const_spec(wub.shape), _const_spec(cw.shape),
                  _const_spec(cb.shape), _const_spec(wd.shape), _const_spec((1, D_MODEL))],
        out_specs=[pl.BlockSpec((tm, D_MODEL), lambda i: (i, 0)),
                   pl.BlockSpec((SUBLANES, D_FF), lambda i: (i, 0))],
        out_shape=[jax.ShapeDtypeStruct((T, D_MODEL), _F32),
                   jax.ShapeDtypeStruct((T // tm * SUBLANES, D_FF), _F32)],
        compiler_params=_params(("arbitrary",)),
        name="ffn_prompt",
    )(h2, h2, x1, wua, wub, cw, cb, wd, g_post)


def _ffn_sample_kernel(h2_ref, x1_ref, sc_ref, wua, wub, cw_ref, cb_ref, wd, g_post, y_o, tail_o):
    db = sc_ref.shape[1]
    nq = h2_ref.shape[0] // db
    h2 = h2_ref[...]
    a0 = _dot(h2, wua[...])
    a1 = jnp.concatenate([sc_ref[1], a0[:(nq - 1) * db]], axis=0)
    a2 = jnp.concatenate([sc_ref[0], sc_ref[1], a0[:(nq - 2) * db]], axis=0)
    cw = cw_ref[...]
    conv = cb_ref[...] + cw[0:1] * a2 + cw[1:2] * a1 + cw[2:3] * a0
    y_o[...] = _ffn_tail(x1_ref[...], conv, _dot(h2, wub[...]), wd[...], g_post[...])
    tail_o[...] = a0[(nq - 2) * db:]


def _ffn_sample(h2, x1, sc, wua, wub, cw, cb, wd, g_post):
    rows = h2.shape[0]
    db = sc.shape[1]
    return pl.pallas_call(
        _ffn_sample_kernel,
        out_shape=[jax.ShapeDtypeStruct((rows, D_MODEL), _F32),
                   jax.ShapeDtypeStruct((2 * db, D_FF), _F32)],
        compiler_params=_params(),
        name="ffn_sample",
    )(h2, x1, sc, wua, wub, cw, cb, wd, g_post)


def _cmp_pages_kernel(pt_ref, *refs):
    n = len(refs) - 3
    w_ref, o_ref, x_scr = refs[n:]
    cpp = PAGE_SIZE // CMP_STRIDE
    pitch = x_scr.shape[1] // CMP_STRIDE
    for kv in range(2):
        for j, r in enumerate(refs[:n]):
            x = jnp.transpose(r[0, kv])
            for c in range(cpp):
                x_scr[kv, pl.ds(j * cpp + c, CMP_STRIDE, stride=pitch), :] = x[c * CMP_STRIDE:(c + 1) * CMP_STRIDE]
        acc = jnp.zeros((n * cpp, 2 * KV_WIDTH), _F32)
        for l in range(CMP_STRIDE):
            acc = acc + _dot(_bf(x_scr[kv, l * pitch:l * pitch + n * cpp, :]), w_ref[l, kv])
        o_ref[:, kv * 2 * KV_WIDTH:(kv + 1) * 2 * KV_WIDTH] = acc


def _slab_pitch(rows):
    tiles = -(-rows // SUBLANES)
    return SUBLANES * (tiles if tiles % 2 else tiles + 1)


def _cmp_pages(page_flat, cache_t, w):
    n_used = page_flat.shape[0]
    cpp = PAGE_SIZE // CMP_STRIDE
    pps = min(PAGES_PER_STEP, n_used)
    page_spec = lambda j: pl.BlockSpec((1,) + cache_t.shape[1:], lambda i, pt: (pt[i * pps + j], 0, 0, 0))
    return pl.pallas_call(
        _cmp_pages_kernel,
        grid_spec=pltpu.PrefetchScalarGridSpec(
            num_scalar_prefetch=1,
            grid=(n_used // pps,),
            in_specs=[page_spec(j) for j in range(pps)] + [pl.BlockSpec(w.shape, lambda i, pt: (0, 0, 0, 0))],
            out_specs=pl.BlockSpec((pps * cpp, 4 * KV_WIDTH), lambda i, pt: (i, 0)),
            scratch_shapes=[pltpu.VMEM((2, CMP_STRIDE * _slab_pitch(pps * cpp), KV_WIDTH), _F32)],
        ),
        out_shape=jax.ShapeDtypeStruct((n_used * cpp, 4 * KV_WIDTH), _F32),
        compiler_params=_params(("arbitrary",)),
        name="cmp_pages",
    )(page_flat, *([cache_t] * pps), w)


def _page_chunk_weights(w1k, w1v):
    G, dh = NSA_KV_GROUPS, HEAD_DIM
    w = jnp.stack([w1k.reshape(2, CMP_STRIDE, dh, dh), w1v.reshape(2, CMP_STRIDE, dh, dh)])
    big = jnp.einsum('ksLde,gh->Lkgdshe', w, jnp.eye(G, dtype=w.dtype))
    return big.reshape(CMP_STRIDE, 2, G * dh, 2 * G * dh)


def _pages_last(cache):
    n, p, kv, h, d = cache.shape
    return jnp.transpose(cache, (0, 2, 3, 4, 1)).reshape(n, kv, h * d, p)


def _softmax_rows(parts):
    m = parts[0].max(axis=1, keepdims=True)
    for s in parts[1:]:
        m = jnp.maximum(m, s.max(axis=1, keepdims=True))
    es = [jnp.exp2(s - m) for s in parts]
    l = es[0].sum(axis=1, keepdims=True)
    for e in es[1:]:
        l = l + e.sum(axis=1, keepdims=True)
    inv = jnp.where(m > MASKED_BELOW, 1.0 / l, 0.0)
    return [e * inv for e in es]


def _nsa_sample_kernel(pt_ref, *refs, n_cmp, n_sel, cur_blocks):
    pps = len(refs) - 23
    pages = refs[:pps]
    (q_ref, gate_ref, parts_ref, win_ref, new_ref, b1k, b1v, w2k, w2v,
     bcmp_ref, bwin_ref, blast_ref, bnew_ref, bfar_ref, c_ref, e_ref,
     o_ref, sel_scr, oc_scr, ow_scr, m_scr, l_scr, acc_scr) = refs[pps:]
    G, HPG, dh = NSA_KV_GROUPS, HEADS_PER_GROUP, HEAD_DIM
    R = q_ref.shape[1]
    GQ = R // HPG
    st = pl.program_id(1)
    n_st = pl.num_programs(1)
    q = q_ref[0]
    n_blk = c_ref.shape[0]
    nk = pps * PAGE_SIZE

    @pl.when(st == 0)
    def _():
        parts = parts_ref[0]
        kc = _bf(_compress_finish(parts[:, 0:KV_WIDTH], parts[:, KV_WIDTH:2 * KV_WIDTH], b1k[...], w2k[...]))
        vc = _bf(_compress_finish(parts[:, 2 * KV_WIDTH:3 * KV_WIDTH], parts[:, 3 * KV_WIDTH:], b1v[...], w2v[...]))
        (p,) = _softmax_rows([_dot_nt(q, kc) + bcmp_ref[...]])
        oc_scr[...] = _dot(_bf(p), vc)
        p_grp = p[0:GQ]
        for h in range(1, HPG):
            p_grp = p_grp + p[h * GQ:(h + 1) * GQ]
        p_pad = jnp.concatenate([p_grp, jnp.zeros((LANES - GQ, p_grp.shape[1]), _F32)], axis=0)
        p_hi, p_lo = _split_hi_lo(p_pad)
        p_slc = _dot_nt(c_ref[...], p_hi) + _dot_nt(c_ref[...], p_lo)

        blk = lax.broadcasted_iota(jnp.int32, (n_blk, 1), 0)
        blk_f = blk.astype(_F32)
        cur = jnp.zeros((1, LANES), jnp.int32)
        qi = lax.broadcasted_iota(jnp.int32, (1, LANES), 1) % (GQ // G)
        for j, cb in enumerate(cur_blocks):
            cur = jnp.where(qi == j, cb, cur)
        forced = (blk == 0) | (blk == cur) | (blk == cur - 1)
        score = jnp.where(forced, FORCE_SCORE, jnp.where(blk <= cur, p_slc, NEG))
        score = jnp.where(blk < n_sel, score, -jnp.inf)
        sel = jnp.zeros((n_blk, LANES), _F32)
        for _ in range(min(SEL_TOP, n_sel)):
            best = jnp.max(score, axis=0, keepdims=True)
            first = jnp.min(jnp.where(score == best, blk_f, float(n_blk)), axis=0, keepdims=True)
            hit = blk_f == first
            sel = jnp.where(hit, 1.0, sel)
            score = jnp.where(hit, -jnp.inf, score)
        sel_scr[...] = sel

        new = new_ref[0]
        pad = jnp.zeros((PAGE_SIZE - new.shape[0], KV_WIDTH), _F32)
        col = lambda j: _bf(jnp.concatenate([new[:, j * KV_WIDTH:(j + 1) * KV_WIDTH], pad], axis=0))
        ks_new, vs_new, kw_new, vw_new = col(2), col(3), col(4), col(5)

        p_w, p_wn = _softmax_rows([_dot(q, _bf(win_ref[0, 0])) + bwin_ref[...],
                                   _dot_nt(q, kw_new) + bnew_ref[...]])
        ow_scr[...] = _dot_nt(_bf(p_w), _bf(win_ref[0, 1])) + _dot(_bf(p_wn), vw_new)

        s_new = _dot_nt(q, ks_new) + bnew_ref[...]
        m0 = s_new.max(axis=1, keepdims=True)
        p_new = jnp.where(s_new > MASKED_BELOW, jnp.exp2(s_new - m0), 0.0)
        m_scr[...] = m0
        l_scr[...] = p_new.sum(axis=1, keepdims=True)
        acc_scr[...] = _dot(_bf(p_new), vs_new)

    keys_t = jnp.concatenate([_bf(r[0, 0]) for r in pages], axis=1)
    vals_t = jnp.concatenate([_bf(r[0, 1]) for r in pages], axis=1)
    bfar = bfar_ref[...]
    is_last = st == n_st - 1
    bias = jnp.concatenate([bfar] * (pps - 1) + [jnp.where(is_last, blast_ref[...], bfar)], axis=1)
    s = _dot(q, keys_t) + bias
    blk_per_step = nk // SEL_BLOCK
    sel_st = _bf(sel_scr[pl.ds(pl.multiple_of(st * blk_per_step, SUBLANES), blk_per_step), :])
    mask_gq = _dot_tn(sel_st, e_ref[...])[0:GQ]
    mask = jnp.concatenate([mask_gq] * HPG, axis=0) > 0.5
    s = jnp.where(mask, s, NEG)
    m_new = jnp.maximum(m_scr[...], s.max(axis=1, keepdims=True))
    alpha = jnp.exp2(m_scr[...] - m_new)
    p = jnp.where(mask, jnp.exp2(s - m_new), 0.0)
    l_scr[...] = alpha * l_scr[...] + p.sum(axis=1, keepdims=True)
    acc_scr[...] = alpha * acc_scr[...] + _dot_nt(_bf(p), vals_t)
    m_scr[...] = m_new

    @pl.when(is_last)
    def _():
        gt = jax.nn.sigmoid(gate_ref[0])
        l_fin = l_scr[...]
        o_sel = acc_scr[...] / jnp.where(l_fin > 0.0, l_fin, 1.0)
        o_ref[0] = gt[:, 0:1] * oc_scr[...] + gt[:, 1:2] * o_sel + gt[:, 2:3] * ow_scr[...]


def _nsa_sample(page_flat, cache, q32, gate32, parts, win, new8, b1k, b1v, w2k, w2v,
                bcmp, bwin, blast, bnew, bfar, cmat, n_cmp, n_sel, cur_blocks):
    db, R, _ = q32.shape
    n_pages = page_flat.shape[0] // db
    pps = min(PAGES_PER_STEP, n_pages)
    n_st = n_pages // pps
    page_spec = lambda j: pl.BlockSpec((1,) + cache.shape[1:],
                                       lambda b, s, pt: (pt[b * n_pages + s * pps + j], 0, 0, 0))
    per_seq = lambda a: pl.BlockSpec((1,) + a.shape[1:], lambda b, s, pt: (b,) + (0,) * (a.ndim - 1))
    const = lambda a: pl.BlockSpec(a.shape, lambda b, s, pt: (0,) * a.ndim)
    kern = functools.partial(_nsa_sample_kernel, n_cmp=n_cmp, n_sel=n_sel, cur_blocks=cur_blocks)
    nk = pps * PAGE_SIZE
    expand = jnp.asarray(np.arange(nk // SEL_BLOCK)[:, None] == np.arange(nk)[None, :] // SEL_BLOCK, _BF16)
    return pl.pallas_call(
        kern,
        grid_spec=pltpu.PrefetchScalarGridSpec(
            num_scalar_prefetch=1,
            grid=(db, n_st),
            in_specs=[page_spec(j) for j in range(pps)]
            + [per_seq(q32), per_seq(gate32), per_seq(parts), per_seq(win), per_seq(new8),
               const(b1k), const(b1v), const(w2k), const(w2v),
               const(bcmp), const(bwin), const(blast), const(bnew), const(bfar), const(cmat), const(expand)],
            out_specs=pl.BlockSpec((1, R, KV_WIDTH), lambda b, s, pt: (b, 0, 0)),
            scratch_shapes=[pltpu.VMEM((cmat.shape[0], LANES), _F32),
                            pltpu.VMEM((R, KV_WIDTH), _F32), pltpu.VMEM((R, KV_WIDTH), _F32),
                            pltpu.VMEM((R, 1), _F32), pltpu.VMEM((R, 1), _F32), pltpu.VMEM((R, KV_WIDTH), _F32)],
        ),
        out_shape=jax.ShapeDtypeStruct((db, R, KV_WIDTH), _F32),
        compiler_params=_params(("arbitrary", "arbitrary")),
        name="nsa_sample",
    )(page_flat, *([cache] * pps), q32, gate32, parts, win, new8, b1k, b1v, w2k, w2v,
      bcmp, bwin, blast, bnew, bfar, cmat, expand)


def _sb_sample_kernel(pt_ref, *refs):
    pps = len(refs) - 7
    pages = refs[:pps]
    q_ref, new_ref, nmask_ref, tt_ref, o_ref, acc_scr, car_scr = refs[pps:]
    st = pl.program_id(1)
    q = q_ref[0]
    tt = tt_ref[...]

    @pl.when(st == 0)
    def _():
        new = new_ref[0]
        pad = jnp.zeros((PAGE_SIZE - new.shape[0], SB_WIDTH), _F32)
        keys = _bf(jnp.concatenate([new[:, :SB_WIDTH], pad], axis=0))
        vals = _bf(jnp.concatenate([new[:, SB_WIDTH:], pad], axis=0))
        (a,), (carry,) = _sb_chunks([_dot_nt(q, keys)], [jnp.zeros(car_scr.shape, _F32)], tt,
                                    nmask_ref[...] > 0.5, True)
        acc_scr[...] = _dot(_bf(a), vals)
        car_scr[...] = carry

    keys_t = jnp.concatenate([_bf(r[0, 0]) for r in pages], axis=1)
    vals_t = jnp.concatenate([_bf(r[0, 1]) for r in pages], axis=1)
    (a,), (carry,) = _sb_chunks([_dot(q, keys_t)], [car_scr[...]], tt, None, True)
    acc_scr[...] = acc_scr[...] + _dot_nt(_bf(a), vals_t)
    car_scr[...] = carry

    @pl.when(st == pl.num_programs(1) - 1)
    def _():
        o_ref[0] = acc_scr[...]


def _sb_sample(page_flat, cache, q32, new8, nmask, tt):
    db, R, _ = q32.shape
    n_pages = page_flat.shape[0] // db
    pps = min(PAGES_PER_STEP, n_pages)
    n_st = n_pages // pps
    page_spec = lambda j: pl.BlockSpec(
        (1,) + cache.shape[1:], lambda b, s, pt: (pt[b * n_pages + n_pages - 1 - (s * pps + j)], 0, 0, 0))
    per_seq = lambda a: pl.BlockSpec((1,) + a.shape[1:], lambda b, s, pt: (b,) + (0,) * (a.ndim - 1))
    const = lambda a: pl.BlockSpec(a.shape, lambda b, s, pt: (0,) * a.ndim)
    return pl.pallas_call(
        _sb_sample_kernel,
        grid_spec=pltpu.PrefetchScalarGridSpec(
            num_scalar_prefetch=1,
            grid=(db, n_st),
            in_specs=[page_spec(j) for j in range(pps)] + [per_seq(q32), per_seq(new8), const(nmask), const(tt)],
            out_specs=pl.BlockSpec((1, R, SB_WIDTH), lambda b, s, pt: (b, 0, 0)),
            scratch_shapes=[pltpu.VMEM((R, SB_WIDTH), _F32), pltpu.VMEM((R, LANES), _F32)],
        ),
        out_shape=jax.ShapeDtypeStruct((db, R, SB_WIDTH), _F32),
        compiler_params=_params(("arbitrary", "arbitrary")),
        name="sb_sample",
    )(page_flat, *([cache] * pps), q32, new8, nmask, tt)


def _split_w_in(w_in):
    sizes = [NSA_WIDTH, 6 * KV_WIDTH, 3 * NSA_HEADS, SB_WIDTH, 2 * SB_WIDTH, 2 * D_MODEL]
    cuts = np.cumsum(sizes)[:-1].tolist()
    wqa, wkv, wgn, wqb, wsb, wgm = jnp.split(w_in, cuts, axis=1)
    wgn = jnp.pad(wgn, ((0, 0), (0, LANES - wgn.shape[1])))
    return [_bf(w) for w in (wqa, wkv, wgn, wqb, wsb, wgm)]


def _bucket_starts():
    b = _bucket_np(np.arange(4 * MAX_DISTANCE))
    assert b.max() == N_BUCKETS - 1 and (np.diff(b) >= 0).all()
    return [int(np.argmax(b >= k)) for k in range(1, N_BUCKETS)]


def _bias_of(dist, rb_ref, head, starts):
    out = jnp.full(dist.shape, rb_ref[N_BUCKETS - 1, head] * LOG2E, _F32)
    for b in range(N_BUCKETS - 2, -1, -1):
        out = jnp.where(dist < starts[b], rb_ref[b, head] * LOG2E, out)
    return out


def _bias_tables_kernel(rb_ref, bc_o, bs_o, bw_o, bfar_o, bfl_o, scmp_o, swin_o, slast_o, snew_o, sfar_o, *,
                        n_cmp_s):
    G, HPG, QT, Q = NSA_KV_GROUPS, HEADS_PER_GROUP, QUERY_TILE, DEC_SEQ
    starts = _bucket_starts()

    qi = lax.broadcasted_iota(jnp.int32, (1, QT), 1)

    def prompt_table(o_ref, dist_fn, valid_fn):
        r = lax.broadcasted_iota(jnp.int32, (o_ref.shape[1], 1), 0)
        dist = dist_fn(r, qi)
        valid = valid_fn(dist)
        for g in range(G):
            for h in range(HPG):
                o_ref[g, :, h * QT:(h + 1) * QT] = jnp.where(valid, _bias_of(dist, rb_ref, g * HPG + h, starts), NEG)

    prompt_table(bc_o, lambda r, i: i - CMP_STRIDE * (r - CMP_PAD) - (CMP_BLOCK - 1), lambda d: d >= 0)
    prompt_table(bs_o, lambda r, i: i + KEY_TILE - r, lambda d: d >= 0)
    prompt_table(bw_o, lambda r, i: i + WINDOW - r, lambda d: (d >= 0) & (d < WINDOW))
    for g in range(G):
        for h in range(HPG):
            bfar_o[g, :, h * QT:(h + 1) * QT] = jnp.full((1, QT), rb_ref[N_BUCKETS - 1, g * HPG + h] * LOG2E, _F32)
    nb = SEL_KEY_CHUNK // SEL_BLOCK
    lane = lax.broadcasted_iota(jnp.int32, (1, LANES), 1)
    bfl_o[...] = jnp.zeros(bfl_o.shape, _F32)
    for g in range(G):
        for h in range(HPG):
            far = jnp.full((1, LANES), rb_ref[N_BUCKETS - 1, g * HPG + h] * LOG2E, _F32)
            hi, lo = _split_hi_lo(far)
            bfl_o[g, h:h + 1, :] = jnp.where(lane == nb, hi.astype(_F32), jnp.where(lane == nb + 1, lo.astype(_F32), 0.0))

    GQ = G * Q
    row = lax.broadcasted_iota(jnp.int32, (GQ, 1), 0)
    row_g = row // Q
    t = PAST_LEN + row % Q

    def by_group(fn):
        out = fn(0)
        for g in range(1, G):
            out = jnp.where(row_g == g, fn(g), out)
        return out

    def sample_table(o_ref, kpos_fn, valid_fn):
        k = lax.broadcasted_iota(jnp.int32, (1, o_ref.shape[1]), 1)
        dist = t - kpos_fn(k)
        valid = valid_fn(dist, k)
        for h in range(HPG):
            b = by_group(lambda g: _bias_of(dist, rb_ref, g * HPG + h, starts))
            o_ref[h * GQ:(h + 1) * GQ, :] = jnp.where(valid, b, NEG)

    w_buf = swin_o.shape[1]
    sample_table(scmp_o, lambda k: CMP_STRIDE * k + CMP_BLOCK - 1, lambda d, k: (d >= 0) & (k < n_cmp_s))
    sample_table(swin_o, lambda k: PAST_LEN - w_buf + k,
                 lambda d, k: (d >= 0) & (d < WINDOW) & (PAST_LEN - w_buf + k >= 0))
    sample_table(slast_o, lambda k: PAST_LEN - PAGE_SIZE + k, lambda d, k: d >= 0)
    sample_table(snew_o, lambda k: PAST_LEN + k, lambda d, k: (d >= 0) & (k < Q))
    for h in range(HPG):
        far = by_group(lambda g: jnp.full((GQ, LANES), rb_ref[N_BUCKETS - 1, g * HPG + h] * LOG2E, _F32))
        sfar_o[h * GQ:(h + 1) * GQ, :] = far


def _bias_tables(rel_bias, n_chunk_s, n_cmp_s, w_buf):
    G, HPG, QT, Q = NSA_KV_GROUPS, HEADS_PER_GROUP, QUERY_TILE, DEC_SEQ
    NL, R = HPG * QT, HPG * G * Q
    f32 = lambda *s: jax.ShapeDtypeStruct(s, _F32)
    return pl.pallas_call(
        functools.partial(_bias_tables_kernel, n_cmp_s=n_cmp_s),
        in_specs=[pl.BlockSpec(memory_space=pltpu.SMEM)],
        out_shape=[f32(G, CMP_WIN, NL), f32(G, 2 * KEY_TILE, NL), f32(G, WINDOW + QT, NL), f32(G, 1, NL),
                   f32(G, SUBLANES, LANES), f32(R, n_chunk_s), f32(R, w_buf), f32(R, PAGE_SIZE), f32(R, PAGE_SIZE), f32(R, LANES)],
        compiler_params=_params(),
        name="bias_tables",
    )(rel_bias)


def kernel(x_prompt, x_sample, cache_cmp, cache_sel, cache_sb, state_win, state_conv, page_table,
           rel_bias, g_pre_mix, w_in, w_ck1, b_ck1, w_ck2, w_cv1, b_cv1, w_cv2, w_branch_a, w_branch_b,
           w_out, g_post_mix, g_pre_ffn, w_ffn_up, conv_w, conv_b, w_ffn_down, g_post_ffn):
    assert DEPTH == 1 and x_prompt.shape[0] == 1
    G, HPG, dh, Q = NSA_KV_GROUPS, HEADS_PER_GROUP, HEAD_DIM, DEC_SEQ
    T = x_prompt.shape[1]
    DB = x_sample.shape[0]
    n_pages = page_table.shape[1]
    w_buf = state_win.shape[2]
    row = lambda v: v.reshape(1, -1)

    w_proj = _split_w_in(w_in[0])
    g_pre = row(g_pre_mix[0])
    w1k, w1v = _bf(_chunk_weights(w_ck1[0])), _bf(_chunk_weights(w_cv1[0]))
    b1k, b1v = row(jnp.tile(b_ck1[0], G)), row(jnp.tile(b_cv1[0], G))
    w2k, w2v = _bf(_group_diag(w_ck2[0])), _bf(_group_diag(w_cv2[0]))
    wa, wb, wo = _bf(w_branch_a[0]), _bf(w_branch_b[0]), _bf(w_out[0])
    wua, wub = _bf(w_ffn_up[0][:, :D_FF]), _bf(w_ffn_up[0][:, D_FF:])
    wd = _bf(w_ffn_down[0])
    cw, cb = conv_w[0], row(conv_b[0])
    g_post_m, g_pre_f, g_post_f = row(g_post_mix[0]), row(g_pre_ffn[0]), row(g_post_ffn[0])
    tt_prompt = jnp.asarray(_cumsum_matrix(latest_first=False), _BF16)
    tt_sample = jnp.asarray(_cumsum_matrix(latest_first=True), _BF16)

    xp = x_prompt[0]
    qa, kv, kvb, gn, qb, sb, sbb, gm = _proj_in(xp, g_pre, w_proj)
    n_chunk = T // CMP_STRIDE
    n_cmp = (T - CMP_BLOCK) // CMP_STRIDE + 1
    kvcol = lambda a, j: a[:, j * KV_WIDTH:(j + 1) * KV_WIDTH]
    kc, vc = _compress_prompt(kvcol(kvb, 0).reshape(n_chunk, CMP_STRIDE * KV_WIDTH),
                              kvcol(kvb, 1).reshape(n_chunk, CMP_STRIDE * KV_WIDTH),
                              w1k, w1v, b1k, b1v, w2k, w2v)
    kcp = jnp.pad(kc, ((CMP_PAD, 0), (0, 0)))
    vcp = jnp.pad(vc, ((CMP_PAD, 0), (0, 0)))
    front = lambda a: jnp.pad(a, ((WINDOW, 0), (0, 0)))
    cpp = PAGE_SIZE // CMP_STRIDE
    n_chunk_s = n_pages * cpp
    n_cmp_s = (PAST_LEN + Q - CMP_BLOCK) // CMP_STRIDE + 1
    bc, bs, bw, bfar, bfl, bcmp, bwin, blast, bnew, bfar_s = _bias_tables(rel_bias, n_chunk_s, n_cmp_s, w_buf)
    ct =jnp.asarray(_cover_matrix(n_chunk + CMP_PAD, T // SEL_BLOCK, n_cmp, CMP_PAD).T, _BF16)
    o_a = _nsa_prompt(qa, gn, kcp, vcp, front(kvcol(kvb, 2)), front(kvcol(kvb, 3)),
                      front(kvcol(kvb, 4)), front(kvcol(kvb, 5)), bc, bs, bw, bfar, bfl, ct)
    o_b = _sb_prompt(qb, sbb, tt_prompt)
    x1, h2 = _merge(xp, o_a, o_b, gm, wa, wb, wo, g_post_m, g_pre_f)
    y_p, tails = _ffn_prompt(h2, x1, wua, wub, cw, cb, wd, g_post_f)

    kv_cmp_p = kv[:, 0:2 * KV_WIDTH].reshape(1, 1, T, 2, G, dh)
    kv_sel_p = kv[:, 2 * KV_WIDTH:4 * KV_WIDTH].reshape(1, 1, T, 2, G, dh)
    kv_sb_p = sb.reshape(1, 1, T, 2, SB_HEADS, dh)
    win_rows = jnp.pad(kv[:, 4 * KV_WIDTH:], ((WINDOW, 0), (0, 0)))[T + WINDOW - w_buf:]
    win_p = win_rows.reshape(1, 1, w_buf, 2, G, dh)
    conv_p = tails[-(CONV_W - 1):].reshape(1, 1, CONV_W - 1, D_FF)

    xs = jnp.transpose(x_sample, (1, 0, 2)).reshape(Q * DB, D_MODEL)
    qa_s, kv_s, _, gn_s, qb_s, sb_s, _, gm_s = _proj_in(xs, g_pre, w_proj)
    by_seq = lambda a: jnp.transpose(a.reshape(Q, DB, -1), (1, 0, 2))
    page_flat = page_table.reshape(-1)

    assert n_cmp_s < n_chunk_s + 1 and Q <= SUBLANES and Q <= CMP_STRIDE
    parts = _cmp_pages(page_flat, _pages_last(cache_cmp[0]), _bf(_page_chunk_weights(w_ck1[0], w_cv1[0])))
    parts = parts.reshape(DB, n_chunk_s, 4 * KV_WIDTH)

    q5 = by_seq(qa_s).reshape(DB, Q, G, HPG, dh)
    q5 = jnp.transpose(q5, (0, 3, 2, 1, 4))
    q32 = jnp.einsum('bhgqd,gk->bhgqkd', q5, jnp.eye(G, dtype=q5.dtype)).reshape(DB, HPG * G * Q, KV_WIDTH)
    g5 = by_seq(gn_s)[:, :, :3 * NSA_HEADS].reshape(DB, Q, 3, G, HPG)
    gate32 = jnp.transpose(g5, (0, 4, 3, 1, 2)).reshape(DB, HPG * G * Q, 3)
    gate32 = jnp.pad(gate32, ((0, 0), (0, 0), (0, LANES - 3)))
    new8 = jnp.pad(by_seq(kv_s), ((0, 0), (0, SUBLANES - Q), (0, 0)))
    n_sel_s = -(-(PAST_LEN + Q) // SEL_BLOCK)
    n_blk_s = -(-n_sel_s // LANES) * LANES
    cmat =jnp.asarray(_cover_matrix(n_chunk_s, n_blk_s, n_cmp_s, 0), _BF16)
    cmat = (cmat * (np.arange(n_blk_s)[None, :] < n_sel_s)).T
    cur_blocks = tuple(int((PAST_LEN + j) // SEL_BLOCK) for j in range(Q))
    o32 = _nsa_sample(page_flat, _pages_last(cache_sel[0]), q32, gate32, parts,
                      _pages_last(state_win[0]), new8, b1k, b1v, w2k, w2v,
                      bcmp, bwin, blast, bnew, bfar_s, cmat, n_cmp_s, n_sel_s, cur_blocks)
    o6 = o32.reshape(DB, HPG, G, Q, G, dh)
    o_a_s = jnp.stack([o6[:, :, g, :, g, :] for g in range(G)], axis=1)
    o_a_s = jnp.transpose(o_a_s, (3, 0, 1, 2, 4)).reshape(Q * DB, NSA_WIDTH)

    qb5 = jnp.transpose(by_seq(qb_s).reshape(DB, Q, SB_HEADS, dh), (0, 2, 1, 3))
    qsb = jnp.einsum('bhqd,hk->bhqkd', qb5, jnp.eye(SB_HEADS, dtype=qb5.dtype)).reshape(DB, SB_HEADS * Q, SB_WIDTH)
    sb_new8 = jnp.pad(by_seq(sb_s), ((0, 0), (0, SUBLANES - Q), (0, 0)))
    rq = np.arange(SB_HEADS * Q)[:, None] % Q
    nmask = jnp.asarray((np.arange(PAGE_SIZE)[None, :] < rq).astype(np.float32))
    osb = _sb_sample(page_flat, _pages_last(cache_sb[0]), qsb, sb_new8, nmask, tt_sample)
    o7 = osb.reshape(DB, SB_HEADS, Q, SB_HEADS, dh)
    o_b_s = jnp.stack([o7[:, h, :, h, :] for h in range(SB_HEADS)], axis=1)
    o_b_s = jnp.transpose(o_b_s, (2, 0, 1, 3)).reshape(Q * DB, SB_WIDTH)

    x1_s, h2_s = _merge(xs, _bf(o_a_s), _bf(o_b_s), gm_s, wa, wb, wo, g_post_m, g_pre_f)
    sc = jnp.transpose(state_conv[0], (1, 0, 2))
    y_s_rows, tail_s = _ffn_sample(h2_s, x1_s, sc, wua, wub, cw, cb, wd, g_post_f)

    y_s = by_seq(y_s_rows)
    kv_seq = by_seq(kv_s)
    kv_cmp_s = kv_seq[:, :, 0:2 * KV_WIDTH].reshape(1, DB, Q, 2, G, dh)
    kv_sel_s = kv_seq[:, :, 2 * KV_WIDTH:4 * KV_WIDTH].reshape(1, DB, Q, 2, G, dh)
    kv_sb_s = by_seq(sb_s).reshape(1, DB, Q, 2, SB_HEADS, dh)
    win_new = kv_seq[:, :, 4 * KV_WIDTH:].reshape(DB, Q, 2, G, dh)
    win_s = jnp.concatenate([state_win[0], win_new], axis=1)[:, -w_buf:][None]
    conv_s = jnp.transpose(tail_s.reshape(CONV_W - 1, DB, D_FF), (1, 0, 2))[None]

    return (y_p[None], y_s, kv_cmp_p, kv_sel_p, kv_sb_p, win_p, conv_p,
            kv_cmp_s, kv_sel_s, kv_sb_s, win_s, conv_s)
```

```python
import functools
import math

import numpy as np
import jax
import jax.numpy as jnp
from jax import lax
from jax.experimental import pallas as pl
from jax.experimental.pallas import tpu as pltpu

D_MODEL = 1024
SEQ = 16384
DEPTH = 1
DEC_BATCH = 128
DEC_SEQ = 4
PAST_LEN = 8192
PAGE_SIZE = 128
HEAD_DIM = 64
NSA_HEADS = 8
NSA_KV_GROUPS = 2
HEADS_PER_GROUP = NSA_HEADS // NSA_KV_GROUPS
SB_HEADS = 8
NSA_WIDTH = NSA_HEADS * HEAD_DIM
SB_WIDTH = SB_HEADS * HEAD_DIM
KV_WIDTH = NSA_KV_GROUPS * HEAD_DIM
CMP_BLOCK = 32
CMP_STRIDE = 16
SEL_BLOCK = 64
SEL_TOP = 16
SEL_COVER_W = (0.5, 1.0, 1.0, 1.0, 0.5)
WINDOW = 512
N_BUCKETS = 32
MAX_DISTANCE = 128
D_FF = 2816
CONV_W = 3
SCALE = HEAD_DIM ** -0.5
LOG2E = math.log2(math.e)
EPS = 1e-6
NEG = -1e30
FORCE_SCORE = 1e9
MASKED_BELOW = -5e29
MASK_PENALTY = -NEG
LANES = 128
SUBLANES = 8
QUERY_TILE = 128
KEY_TILE = 128
ROW_TILE = 256
CMP_PAD = 16
CMP_WIN = 24
PAGES_PER_STEP = 16
SB_QUERY_TILE = 256
SB_KEY_CHUNK = 1024
SEL_KEY_CHUNK = 512
VMEM_LIMIT = 56 * 1024 * 1024

_F32 = jnp.float32
_BF16 = jnp.bfloat16


def _bf(x):
    return x.astype(_BF16)


def _dot(a, b):
    return jnp.dot(a, b, preferred_element_type=_F32)


def _dot_nt(a, b):
    return lax.dot_general(a, b, (((1,), (1,)), ((), ())), preferred_element_type=_F32)


def _dot_tn(a, b):
    return lax.dot_general(a, b, (((0,), (0,)), ((), ())), preferred_element_type=_F32)


def _split_hi_lo(x):
    hi = _bf(x)
    lo = _bf(x - hi.astype(_F32))
    return hi, lo


def _rms(x, g):
    return x * lax.rsqrt(jnp.mean(x * x, axis=-1, keepdims=True) + EPS) * g


def _softplus_log2(z2):
    return jnp.maximum(z2, 0.0) + jnp.log2(1.0 + jnp.exp2(-jnp.abs(z2)))


def _const_spec(shape):
    nd = len(shape)
    return pl.BlockSpec(shape, lambda *_: (0,) * nd)


def _params(sem=None):
    return pltpu.CompilerParams(dimension_semantics=sem, vmem_limit_bytes=VMEM_LIMIT)


def _bucket_np(dist):
    n = np.maximum(dist, 0)
    max_exact = N_BUCKETS // 2
    nf = np.maximum(n, 1).astype(np.float32)
    large = max_exact + (np.log(nf / max_exact) / math.log(MAX_DISTANCE / max_exact)
                         * (N_BUCKETS - max_exact)).astype(np.int32)
    return np.where(n < max_exact, n, np.minimum(large, N_BUCKETS - 1)).astype(np.int32)


def _cover_matrix(n_rows, n_cols, n_cmp, row_offset):
    r = np.arange(n_rows)[:, None] - row_offset
    j = np.arange(n_cols)[None, :]
    k = r - 4 * j + 1
    w = np.asarray(SEL_COVER_W, np.float32)
    ok = (k >= 0) & (k <= 4) & (r >= 0) & (r < n_cmp)
    return np.where(ok, w[np.clip(k, 0, 4)], 0.0).astype(np.float32)


def _cumsum_matrix(latest_first):
    k = np.arange(2 * LANES)
    same = (k[:, None] // LANES) == (k[None, :] // LANES)
    within = same & (k[:, None] >= k[None, :])
    if latest_first:
        cross = (k[:, None] < LANES) & (k[None, :] >= LANES)
    else:
        cross = (k[:, None] >= LANES) & (k[None, :] < LANES)
    return (within | cross).astype(np.float32)


def _proj_in_kernel(x_ref, g_ref, wqa, wkv, wgn, wqb, wsb, wgm,
                    qa_o, kv_o, kvb_o, gn_o, qb_o, sb_o, sbb_o, gm_o):
    h = _bf(_rms(x_ref[...], g_ref[...]))
    qa_o[...] = _bf(_dot(h, wqa[...]) * (SCALE * LOG2E))
    kv = _dot(h, wkv[...])
    kv_o[...] = kv
    kvb_o[...] = _bf(kv)
    gn_o[...] = _dot(h, wgn[...])
    qb_o[...] = _bf(_dot(h, wqb[...]) * (SCALE * LOG2E))
    sb = _dot(h, wsb[...])
    sb_o[...] = sb
    sbb_o[...] = _bf(sb)
    gm_o[...] = _dot(h, wgm[...])


def _proj_in(x, g, ws):
    rows = x.shape[0]
    tm = min(ROW_TILE, rows)
    widths = [w.shape[1] for w in ws]
    out_shape = [
        jax.ShapeDtypeStruct((rows, widths[0]), _BF16),
        jax.ShapeDtypeStruct((rows, widths[1]), _F32),
        jax.ShapeDtypeStruct((rows, widths[1]), _BF16),
        jax.ShapeDtypeStruct((rows, widths[2]), _F32),
        jax.ShapeDtypeStruct((rows, widths[3]), _BF16),
        jax.ShapeDtypeStruct((rows, widths[4]), _F32),
        jax.ShapeDtypeStruct((rows, widths[4]), _BF16),
        jax.ShapeDtypeStruct((rows, widths[5]), _F32),
    ]
    row_spec = lambda w: pl.BlockSpec((tm, w), lambda i: (i, 0))
    return pl.pallas_call(
        _proj_in_kernel,
        grid=(rows // tm,),
        in_specs=[row_spec(D_MODEL), _const_spec((1, D_MODEL))] + [_const_spec(w.shape) for w in ws],
        out_specs=[row_spec(s.shape[1]) for s in out_shape],
        out_shape=out_shape,
        compiler_params=_params(("arbitrary",)),
        name="proj_in",
    )(x, g, *ws)


def _compress_finish(parts_lo, parts_hi, b1, w2):
    n = parts_hi.shape[0]
    hi_next = pltpu.roll(parts_hi, n - 1, 0)
    hid = parts_lo + hi_next + b1
    return _dot(_bf(jax.nn.gelu(hid)), w2)


def _compress_kernel(ck_ref, cv_ref, w1k, w1v, b1k, b1v, w2k, w2v, kc_o, vc_o):
    for c_ref, w1, b1, w2, o in ((ck_ref, w1k, b1k, w2k, kc_o), (cv_ref, w1v, b1v, w2v, vc_o)):
        parts = _dot(c_ref[...], w1[...])
        o[...] = _bf(_compress_finish(parts[:, :KV_WIDTH], parts[:, KV_WIDTH:], b1[...], w2[...]))


def _compress_prompt(ck, cv, w1k, w1v, b1k, b1v, w2k, w2v):
    n = ck.shape[0]
    return pl.pallas_call(
        _compress_kernel,
        out_shape=[jax.ShapeDtypeStruct((n, KV_WIDTH), _BF16)] * 2,
        compiler_params=_params(),
        name="compress_prompt",
    )(ck, cv, w1k, w1v, b1k, b1v, w2k, w2v)


def _chunk_weights(w1):
    G, dh = NSA_KV_GROUPS, HEAD_DIM
    w1r = w1.reshape(2, CMP_STRIDE, dh, dh)
    eye = jnp.eye(G, dtype=w1.dtype)
    big = jnp.einsum('slde,gh->lgdshe', w1r, eye)
    return big.reshape(CMP_STRIDE * G * dh, 2 * G * dh)


def _group_diag(w2):
    return jnp.kron(jnp.eye(NSA_KV_GROUPS, dtype=w2.dtype), w2)


def _nsa_prompt_kernel(qa_ref, gn_ref, kcp_ref, vcp_ref, ks_ref, vs_ref, kw_ref, vw_ref,
                       bc_ref, bs_ref, bw_ref, bfar_ref, bfl_ref, ct_ref, pat_ref, pick_ref,
                       oa_ref, s_scr, sel_scr, q_scr, pen_scr, sa_scr, sb_scr):
    G, HPG, dh = NSA_KV_GROUPS, HEADS_PER_GROUP, HEAD_DIM
    QT = QUERY_TILE
    NL = HPG * QT
    qb = pl.program_id(0)
    c = qb * QT
    n_cp = kcp_ref.shape[0]
    n_blk = ct_ref.shape[0]

    lane_half = lax.broadcasted_iota(jnp.int32, (1, LANES), 1) // dh
    gates_t = jnp.transpose(jax.nn.sigmoid(gn_ref[...]))

    def gate_row(branch, g):
        base = branch * NSA_HEADS + g * HPG
        return jnp.concatenate([gates_t[base + h:base + h + 1, :] for h in range(HPG)], axis=1)

    ti = c + lax.broadcasted_iota(jnp.int32, (1, QT), 1)
    cur = ti // SEL_BLOCK

    for g in range(G):
        q_rows = []
        for h in range(HPG):
            col = (g * HPG + h) * dh
            tile = qa_ref[:, (col // LANES) * LANES:(col // LANES + 1) * LANES].astype(_F32)
            if (col % LANES) // dh != g:
                tile = pltpu.roll(tile, dh, 1)
            q_rows.append(_bf(jnp.where(lane_half == g, tile, 0.0)))
        q_scr[g] = jnp.concatenate(q_rows, axis=0)

    o_cmp = [None] * G
    for g in range(G):
        qg = q_scr[g]
        bfar = bfar_ref[g]

        r0 = pl.multiple_of(qb * (QT // CMP_STRIDE), SUBLANES)
        s_scr[...] = _dot_nt(kcp_ref[...], qg) + bfar
        s_scr[pl.ds(r0, CMP_WIN), :] = s_scr[pl.ds(r0, CMP_WIN), :] - bfar + bc_ref[g]
        row = lax.broadcasted_iota(jnp.int32, (n_cp, 1), 0)
        s = jnp.where((row >= CMP_PAD) & (row < r0 + CMP_WIN), s_scr[...], NEG)
        m = jnp.max(s, axis=0, keepdims=True)
        e = jnp.exp2(s - m)
        p = e * jnp.where(m > MASKED_BELOW, 1.0 / jnp.sum(e, axis=0, keepdims=True), 0.0)
        o_cmp[g] = _dot_tn(vcp_ref[...], _bf(p))[g * dh:(g + 1) * dh]
        p_grp = p[:, 0:QT]
        for h in range(1, HPG):
            p_grp = p_grp + p[:, h * QT:(h + 1) * QT]
        p_hi, p_lo = _split_hi_lo(p_grp)
        p_slc = _dot(ct_ref[...], p_hi) + _dot(ct_ref[...], p_lo)

        blk = lax.broadcasted_iota(jnp.int32, (n_blk, 1), 0)
        blk_f = blk.astype(_F32)
        forced = (blk == 0) | (blk == cur) | (blk == cur - 1)
        score = jnp.where(forced, FORCE_SCORE, jnp.where(blk <= cur, p_slc, NEG))
        for _ in range(min(SEL_TOP, n_blk)):
            best = jnp.max(score, axis=0, keepdims=True)
            first = jnp.min(jnp.where(score == best, blk_f, float(n_blk)), axis=0, keepdims=True)
            score = jnp.where(blk_f == first, -jnp.inf, score)
        sel = jnp.where(score == -jnp.inf, 1.0, 0.0)
        sel_scr[g] = jnp.concatenate([sel] * HPG, axis=1)
        selm = _bf((jnp.transpose(sel) - 1.0) * MASK_PENALTY)
        pen_scr[g] = _dot(selm, pick_ref[...])

    def sel_rows(g, first_blk, n):
        rows = [jnp.broadcast_to(sel_scr[g, pl.ds(first_blk + j, 1), :], (SEL_BLOCK, NL)) for j in range(n)]
        return jnp.concatenate(rows, axis=0) > 0.5

    def sel_step(g, vals, s_tile, mask, carry, zero_masked):
        m_run, l_run, acc = carry
        if mask is not None:
            s_tile = jnp.where(mask, s_tile, NEG)
        m_new = jnp.maximum(m_run, jnp.max(s_tile, axis=0, keepdims=True))
        alpha = jnp.exp2(m_run - m_new)
        p_t = jnp.exp2(s_tile - m_new)
        if zero_masked:
            p_t = jnp.where(mask, p_t, 0.0)
        l_new = alpha * l_run + jnp.sum(p_t, axis=0, keepdims=True)
        pv = _dot_tn(vals, _bf(p_t))[g * dh:(g + 1) * dh]
        return m_new, l_new, alpha * acc + pv

    n_far_blk = jnp.maximum(qb - 1, 0) * (KEY_TILE // SEL_BLOCK)
    blk_per_chunk = SEL_KEY_CHUNK // SEL_BLOCK
    lane = lax.broadcasted_iota(jnp.int32, (1, LANES), 1)

    n_far_chunks = (n_far_blk + blk_per_chunk - 1) // blk_per_chunk

    def chunk_rows(f):
        return pl.multiple_of(f * SEL_KEY_CHUNK + WINDOW, KEY_TILE)

    def far_scores(f, out_scr):
        rows = chunk_rows(jnp.minimum(f, jnp.maximum(n_far_chunks - 1, 0)))
        keys = jnp.concatenate([ks_ref[pl.ds(rows, SEL_KEY_CHUNK), :], pat_ref[...]], axis=1)
        blk_of_lane = f * blk_per_chunk + lane
        off = jnp.where((lane < blk_per_chunk) & (blk_of_lane >= n_far_blk), -MASK_PENALTY, 0.0)
        col = pl.multiple_of(jnp.minimum(f, n_blk // blk_per_chunk - 1) * LANES, LANES)
        for g in range(G):
            pen = pen_scr[g, :, pl.ds(col, LANES)] + off
            extra = jnp.concatenate([_bf(pen + bfl_ref[g, h:h + 1, :]) for h in range(HPG)], axis=0)
            q_aug = jnp.concatenate([q_scr[g], extra], axis=1)
            out_scr[g] = _dot_nt(keys, q_aug)

    def far_consume(f, in_scr, state):
        vals = vs_ref[pl.ds(chunk_rows(jnp.minimum(f, jnp.maximum(n_far_chunks - 1, 0))), SEL_KEY_CHUNK), :]
        return tuple(sel_step(g, vals, in_scr[g], None, state[g], False) for g in range(G))

    def far_body(k, state):
        far_scores(2 * k + 1, sb_scr)
        state = far_consume(2 * k, sa_scr, state)
        far_scores(2 * k + 2, sa_scr)
        return far_consume(2 * k + 1, sb_scr, state)

    init = tuple((jnp.full((1, NL), NEG, _F32), jnp.zeros((1, NL), _F32), jnp.zeros((dh, NL), _F32))
                 for _ in range(G))
    far_scores(0, sa_scr)
    far = lax.fori_loop(0, (n_far_chunks + 1) // 2, far_body, init)

    for g in range(G):
        qg = q_scr[g]
        rn = pl.multiple_of(c + WINDOW - KEY_TILE, KEY_TILE)
        keys = ks_ref[pl.ds(rn, 2 * KEY_TILE), :]
        vals = vs_ref[pl.ds(rn, 2 * KEY_TILE), :]
        bias_near = bs_ref[g]
        s_tile = _dot_nt(keys, qg) + bias_near
        kpos = c - KEY_TILE + lax.broadcasted_iota(jnp.int32, (2 * KEY_TILE, 1), 0)
        first_blk = jnp.maximum(qb - 1, 0) * (KEY_TILE // SEL_BLOCK)
        near_sel = jnp.concatenate([
            jnp.where(qb > 0, 1.0, 0.0) * sel_rows(g, first_blk, KEY_TILE // SEL_BLOCK).astype(_F32),
            sel_rows(g, qb * (KEY_TILE // SEL_BLOCK), KEY_TILE // SEL_BLOCK).astype(_F32)], axis=0) > 0.5
        mask = near_sel & (bias_near > MASKED_BELOW) & (kpos >= 0)
        _, l_fin, acc = sel_step(g, vals, s_tile, mask, far[g], True)
        o_sel = acc * (1.0 / jnp.where(l_fin > 0.0, l_fin, 1.0))

        rw = pl.multiple_of(c, KEY_TILE)
        n_w = WINDOW + QT
        bias_w = bw_ref[g]
        s_w = _dot_nt(kw_ref[pl.ds(rw, n_w), :], qg) + bias_w
        kpos_w = c - WINDOW + lax.broadcasted_iota(jnp.int32, (n_w, 1), 0)
        mask_w = (bias_w > MASKED_BELOW) & (kpos_w >= 0)
        s_w = jnp.where(mask_w, s_w, NEG)
        m_w = jnp.max(s_w, axis=0, keepdims=True)
        e_w = jnp.exp2(s_w - m_w)
        p_w = e_w * jnp.where(m_w > MASKED_BELOW, 1.0 / jnp.sum(e_w, axis=0, keepdims=True), 0.0)
        o_win = _dot_tn(vw_ref[pl.ds(rw, n_w), :], _bf(p_w))[g * dh:(g + 1) * dh]

        o_t = gate_row(0, g) * o_cmp[g] + gate_row(1, g) * o_sel + gate_row(2, g) * o_win
        for pair in range(HPG // 2):
            blk_t = jnp.concatenate([o_t[:, (2 * pair) * QT:(2 * pair + 1) * QT],
                                     o_t[:, (2 * pair + 1) * QT:(2 * pair + 2) * QT]], axis=0)
            col = (g * HPG + 2 * pair) * dh
            oa_ref[:, col:col + LANES] = _bf(jnp.transpose(blk_t))


def _chunk_pattern():
    nb = SEL_KEY_CHUNK // SEL_BLOCK
    pat = np.zeros((SEL_KEY_CHUNK, LANES), np.float32)
    pat[np.arange(SEL_KEY_CHUNK), np.arange(SEL_KEY_CHUNK) // SEL_BLOCK] = 1.0
    pat[:, nb:nb + 2] = 1.0
    return pat


def _nsa_prompt(qa, gn, kcp, vcp, ks, vs, kw, vw, bc, bs, bw, bfar, bfl, ct):
    T = qa.shape[0]
    QT = QUERY_TILE
    NL = HEADS_PER_GROUP * QT
    pat = jnp.asarray(_chunk_pattern(), _BF16)
    n_blk = ct.shape[0]
    nb = SEL_KEY_CHUNK // SEL_BLOCK
    b = np.arange(n_blk)
    pick = np.zeros((n_blk, n_blk // nb * LANES), np.float32)
    pick[b, b // nb * LANES + b % nb] = 1.0
    pick = jnp.asarray(pick, _BF16)
    full = lambda a: _const_spec(a.shape)
    return pl.pallas_call(
        _nsa_prompt_kernel,
        grid=(T // QT,),
        in_specs=[pl.BlockSpec((QT, NSA_WIDTH), lambda i: (i, 0)),
                  pl.BlockSpec((QT, LANES), lambda i: (i, 0)),
                  full(kcp), full(vcp), full(ks), full(vs), full(kw), full(vw),
                  full(bc), full(bs), full(bw), full(bfar), full(bfl), full(ct), full(pat), full(pick)],
        out_specs=pl.BlockSpec((QT, NSA_WIDTH), lambda i: (i, 0)),
        out_shape=jax.ShapeDtypeStruct((T, NSA_WIDTH), _BF16),
        scratch_shapes=[pltpu.VMEM((kcp.shape[0], NL), _F32),
                        pltpu.VMEM((NSA_KV_GROUPS, n_blk, NL), _F32),
                        pltpu.VMEM((NSA_KV_GROUPS, NL, LANES), _BF16),
                        pltpu.VMEM((NSA_KV_GROUPS, QT, pick.shape[1]), _F32),
                        pltpu.VMEM((NSA_KV_GROUPS, SEL_KEY_CHUNK, NL), _F32),
                        pltpu.VMEM((NSA_KV_GROUPS, SEL_KEY_CHUNK, NL), _F32)],
        compiler_params=_params(("arbitrary",)),
        name="nsa_prompt",
    )(qa, gn, kcp, vcp, ks, vs, kw, vw, bc, bs, bw, bfar, bfl, ct, pat, pick)


def _sb_chunks(zs, carries, tt, mask, latest_first):
    width = zs[0].shape[1]
    step = min(2 * LANES, width)
    tri = tt if step == 2 * LANES else tt[:LANES, :LANES]
    n = width // step
    sps = []
    for z in zs:
        sp = _softplus_log2(z)
        sps.append(sp if mask is None else jnp.where(mask, sp, 0.0))
    carries = list(carries)
    a_parts = [[None] * n for _ in zs]
    for j in (range(n) if latest_first else reversed(range(n))):
        sl = slice(j * step, (j + 1) * step)
        css = [_dot(_bf(sp[:, sl]), tri) for sp in sps]
        for i, z in enumerate(zs):
            later = jnp.concatenate([carries[i]] * (step // LANES), axis=1)
            a_parts[i][j] = jnp.exp2(z[:, sl] - css[i] - later)
            carries[i] = carries[i] + jnp.sum(sps[i][:, sl], axis=1, keepdims=True)
    out = []
    for pieces in a_parts:
        a = pieces[0] if n == 1 else jnp.concatenate(pieces, axis=1)
        out.append(a if mask is None else jnp.where(mask, a, 0.0))
    return out, carries


def _sb_prompt_kernel(q_ref, k_ref, v_ref, tt_ref, o_ref, acc_scr, car_scr):
    dh = HEAD_DIM
    QT = q_ref.shape[0]
    KW = min(SB_KEY_CHUNK, k_ref.shape[0])
    i = pl.program_id(1)
    lane_half = lax.broadcasted_iota(jnp.int32, (1, LANES), 1) // dh
    q = q_ref[...].astype(_F32)
    qh = [_bf(jnp.where(lane_half == h, q, 0.0)) for h in range(2)]
    tt = tt_ref[...]
    last = (i * QT + QT - 1) // KW

    def chunk(ci, mask, width=KW):
        r = pl.multiple_of(ci * KW, KW)
        keys = k_ref[pl.ds(r, width), :]
        vals = v_ref[pl.ds(r, width), :]
        zs = [_dot_nt(qh[h], keys) for h in range(2)]
        ws, carries = _sb_chunks(zs, [car_scr[h] for h in range(2)], tt, mask, latest_first=False)
        for h in range(2):
            acc_scr[h] = acc_scr[h] + _dot(_bf(ws[h]), vals)
            car_scr[h] = carries[h]

    acc_scr[...] = jnp.zeros(acc_scr.shape, _F32)
    car_scr[...] = jnp.zeros(car_scr.shape, _F32)
    t = i * QT + lax.broadcasted_iota(jnp.int32, (QT, 1), 0)
    tiles_per_chunk = KW // QT
    for w in range(1, tiles_per_chunk + 1):
        @pl.when(i % tiles_per_chunk == w - 1)
        def _(w=w):
            kpos = last * KW + lax.broadcasted_iota(jnp.int32, (1, w * QT), 1)
            chunk(last, kpos < t, w * QT)

    def body(n, _):
        chunk(last - 1 - n, None)
        return 0

    lax.fori_loop(0, last, body, 0)
    o_ref[...] = _bf(jnp.where(lane_half == 0, acc_scr[0], acc_scr[1]))


def _sb_prompt(qb, sbb, tt):
    T = qb.shape[0]
    QT = min(SB_QUERY_TILE, T)
    n_pair = SB_WIDTH // LANES
    return pl.pallas_call(
        _sb_prompt_kernel,
        grid=(n_pair, T // QT),
        in_specs=[pl.BlockSpec((QT, LANES), lambda p, i: (i, p)),
                  pl.BlockSpec((T, LANES), lambda p, i: (0, p)),
                  pl.BlockSpec((T, LANES), lambda p, i: (0, n_pair + p)),
                  _const_spec(tt.shape)],
        out_specs=pl.BlockSpec((QT, LANES), lambda p, i: (i, p)),
        out_shape=jax.ShapeDtypeStruct((T, SB_WIDTH), _BF16),
        scratch_shapes=[pltpu.VMEM((2, QT, LANES), _F32), pltpu.VMEM((2, QT, LANES), _F32)],
        compiler_params=_params(("arbitrary", "arbitrary")),
        name="sb_prompt",
    )(qb, sbb, sbb, tt)


def _merge_kernel(x_ref, oa_ref, ob_ref, gm_ref, wa, wb, wo, g_post, g_pre, x1_o, h2_o):
    gm = gm_ref[...]
    mixed = (jax.nn.sigmoid(gm[:, :D_MODEL]) * _dot(oa_ref[...], wa[...])
             + jax.nn.sigmoid(gm[:, D_MODEL:]) * _dot(ob_ref[...], wb[...]))
    x1 = x_ref[...] + _rms(_dot(_bf(mixed), wo[...]), g_post[...])
    x1_o[...] = x1
    h2_o[...] = _bf(_rms(x1, g_pre[...]))


def _merge(x, oa, ob, gm, wa, wb, wo, g_post, g_pre):
    rows = x.shape[0]
    tm = min(ROW_TILE, rows)
    row_spec = lambda w: pl.BlockSpec((tm, w), lambda i: (i, 0))
    return pl.pallas_call(
        _merge_kernel,
        grid=(rows // tm,),
        in_specs=[row_spec(D_MODEL), row_spec(NSA_WIDTH), row_spec(SB_WIDTH), row_spec(2 * D_MODEL),
                  _const_spec(wa.shape), _const_spec(wb.shape), _const_spec(wo.shape),
                  _const_spec((1, D_MODEL)), _const_spec((1, D_MODEL))],
        out_specs=[row_spec(D_MODEL), row_spec(D_MODEL)],
        out_shape=[jax.ShapeDtypeStruct((rows, D_MODEL), _F32), jax.ShapeDtypeStruct((rows, D_MODEL), _BF16)],
        compiler_params=_params(("arbitrary",)),
        name="merge",
    )(x, oa, ob, gm, wa, wb, wo, g_post, g_pre)


def _ffn_tail(x1, conv, gate, wd, g_post):
    f = jax.nn.gelu(conv) * gate
    return x1 + _rms(_dot(_bf(f), wd), g_post)


def _ffn_prompt_kernel(h2_ref, halo_ref, x1_ref, wua, wub, cw_ref, cb_ref, wd, g_post, y_o, tail_o):
    i = pl.program_id(0)
    tm = h2_ref.shape[0]
    h2 = h2_ref[...]
    a_ext = _dot(jnp.concatenate([halo_ref[...], h2], axis=0), wua[...])
    row = lax.broadcasted_iota(jnp.int32, (SUBLANES + tm, 1), 0)
    a_ext = jnp.where((row < SUBLANES) & (i == 0), 0.0, a_ext)
    a0 = a_ext[SUBLANES:]
    a1 = pltpu.roll(a_ext, 1, 0)[SUBLANES:]
    a2 = pltpu.roll(a_ext, 2, 0)[SUBLANES:]
    cw = cw_ref[...]
    conv = cb_ref[...] + cw[0:1] * a2 + cw[1:2] * a1 + cw[2:3] * a0
    y_o[...] = _ffn_tail(x1_ref[...], conv, _dot(h2, wub[...]), wd[...], g_post[...])
    tail_o[...] = a0[tm - SUBLANES:]


def _ffn_prompt(h2, x1, wua, wub, cw, cb, wd, g_post):
    T = h2.shape[0]
    tm = min(ROW_TILE, T)
    hb = tm // SUBLANES
    return pl.pallas_call(
        _ffn_prompt_kernel,
        grid=(T // tm,),
        in_specs=[pl.BlockSpec((tm, D_MODEL), lambda i: (i, 0)),
                  pl.BlockSpec((SUBLANES, D_MODEL), lambda i: (jnp.maximum(i * hb - 1, 0), 0)),
                  pl.BlockSpec((tm, D_MODEL), lambda i: (i, 0)),
                  _const_spec(wua.shape), _const_spec(wub.shape), _const_spec(cw.shape),
                  _const_spec(cb.shape), _const_spec(wd.shape), _const_spec((1, D_MODEL))],
        out_specs=[pl.BlockSpec((tm, D_MODEL), lambda i: (i, 0)),
                   pl.BlockSpec((SUBLANES, D_FF), lambda i: (i, 0))],
        out_shape=[jax.ShapeDtypeStruct((T, D_MODEL), _F32),
                   jax.ShapeDtypeStruct((T // tm * SUBLANES, D_FF), _F32)],
        compiler_params=_params(("arbitrary",)),
        name="ffn_prompt",
    )(h2, h2, x1, wua, wub, cw, cb, wd, g_post)


def _ffn_sample_kernel(h2_ref, x1_ref, sc_ref, wua, wub, cw_ref, cb_ref, wd, g_post, y_o, tail_o):
    db = sc_ref.shape[1]
    nq = h2_ref.shape[0] // db
    h2 = h2_ref[...]
    a0 = _dot(h2, wua[...])
    a1 = jnp.concatenate([sc_ref[1], a0[:(nq - 1) * db]], axis=0)
    a2 = jnp.concatenate([sc_ref[0], sc_ref[1], a0[:(nq - 2) * db]], axis=0)
    cw = cw_ref[...]
    conv = cb_ref[...] + cw[0:1] * a2 + cw[1:2] * a1 + cw[2:3] * a0
    y_o[...] = _ffn_tail(x1_ref[...], conv, _dot(h2, wub[...]), wd[...], g_post[...])
    tail_o[...] = a0[(nq - 2) * db:]


def _ffn_sample(h2, x1, sc, wua, wub, cw, cb, wd, g_post):
    rows = h2.shape[0]
    db = sc.shape[1]
    return pl.pallas_call(
        _ffn_sample_kernel,
        out_shape=[jax.ShapeDtypeStruct((rows, D_MODEL), _F32),
                   jax.ShapeDtypeStruct((2 * db, D_FF), _F32)],
        compiler_params=_params(),
        name="ffn_sample",
    )(h2, x1, sc, wua, wub, cw, cb, wd, g_post)


def _cmp_pages_kernel(pt_ref, *refs):
    n = len(refs) - 3
    w_ref, o_ref, x_scr = refs[n:]
    cpp = PAGE_SIZE // CMP_STRIDE
    pitch = x_scr.shape[1] // CMP_STRIDE
    for kv in range(2):
        for j, r in enumerate(refs[:n]):
            x = jnp.transpose(r[0, kv])
            for c in range(cpp):
                x_scr[kv, pl.ds(j * cpp + c, CMP_STRIDE, stride=pitch), :] = x[c * CMP_STRIDE:(c + 1) * CMP_STRIDE]
        acc = jnp.zeros((n * cpp, 2 * KV_WIDTH), _F32)
        for l in range(CMP_STRIDE):
            acc = acc + _dot(_bf(x_scr[kv, l * pitch:l * pitch + n * cpp, :]), w_ref[l, kv])
        o_ref[:, kv * 2 * KV_WIDTH:(kv + 1) * 2 * KV_WIDTH] = acc


def _slab_pitch(rows):
    tiles = -(-rows // SUBLANES)
    return SUBLANES * (tiles if tiles % 2 else tiles + 1)


def _cmp_pages(page_flat, cache_t, w):
    n_used = page_flat.shape[0]
    cpp = PAGE_SIZE // CMP_STRIDE
    pps = min(PAGES_PER_STEP, n_used)
    page_spec = lambda j: pl.BlockSpec((1,) + cache_t.shape[1:], lambda i, pt: (pt[i * pps + j], 0, 0, 0))
    return pl.pallas_call(
        _cmp_pages_kernel,
        grid_spec=pltpu.PrefetchScalarGridSpec(
            num_scalar_prefetch=1,
            grid=(n_used // pps,),
            in_specs=[page_spec(j) for j in range(pps)] + [pl.BlockSpec(w.shape, lambda i, pt: (0, 0, 0, 0))],
            out_specs=pl.BlockSpec((pps * cpp, 4 * KV_WIDTH), lambda i, pt: (i, 0)),
            scratch_shapes=[pltpu.VMEM((2, CMP_STRIDE * _slab_pitch(pps * cpp), KV_WIDTH), _F32)],
        ),
        out_shape=jax.ShapeDtypeStruct((n_used * cpp, 4 * KV_WIDTH), _F32),
        compiler_params=_params(("arbitrary",)),
        name="cmp_pages",
    )(page_flat, *([cache_t] * pps), w)


def _page_chunk_weights(w1k, w1v):
    G, dh = NSA_KV_GROUPS, HEAD_DIM
    w = jnp.stack([w1k.reshape(2, CMP_STRIDE, dh, dh), w1v.reshape(2, CMP_STRIDE, dh, dh)])
    big = jnp.einsum('ksLde,gh->Lkgdshe', w, jnp.eye(G, dtype=w.dtype))
    return big.reshape(CMP_STRIDE, 2, G * dh, 2 * G * dh)


def _pages_last(cache):
    n, p, kv, h, d = cache.shape
    return jnp.transpose(cache, (0, 2, 3, 4, 1)).reshape(n, kv, h * d, p)


def _softmax_rows(parts):
    m = parts[0].max(axis=1, keepdims=True)
    for s in parts[1:]:
        m = jnp.maximum(m, s.max(axis=1, keepdims=True))
    es = [jnp.exp2(s - m) for s in parts]
    l = es[0].sum(axis=1, keepdims=True)
    for e in es[1:]:
        l = l + e.sum(axis=1, keepdims=True)
    inv = jnp.where(m > MASKED_BELOW, 1.0 / l, 0.0)
    return [e * inv for e in es]


def _nsa_sample_kernel(pt_ref, *refs, n_cmp, n_sel, cur_blocks):
    pps = len(refs) - 23
    pages = refs[:pps]
    (q_ref, gate_ref, parts_ref, win_ref, new_ref, b1k, b1v, w2k, w2v,
     bcmp_ref, bwin_ref, blast_ref, bnew_ref, bfar_ref, c_ref, e_ref,
     o_ref, sel_scr, oc_scr, ow_scr, m_scr, l_scr, acc_scr) = refs[pps:]
    G, HPG, dh = NSA_KV_GROUPS, HEADS_PER_GROUP, HEAD_DIM
    R = q_ref.shape[1]
    GQ = R // HPG
    st = pl.program_id(1)
    n_st = pl.num_programs(1)
    q = q_ref[0]
    n_blk = c_ref.shape[0]
    nk = pps * PAGE_SIZE

    @pl.when(st == 0)
    def _():
        parts = parts_ref[0]
        kc = _bf(_compress_finish(parts[:, 0:KV_WIDTH], parts[:, KV_WIDTH:2 * KV_WIDTH], b1k[...], w2k[...]))
        vc = _bf(_compress_finish(parts[:, 2 * KV_WIDTH:3 * KV_WIDTH], parts[:, 3 * KV_WIDTH:], b1v[...], w2v[...]))
        (p,) = _softmax_rows([_dot_nt(q, kc) + bcmp_ref[...]])
        oc_scr[...] = _dot(_bf(p), vc)
        p_grp = p[0:GQ]
        for h in range(1, HPG):
            p_grp = p_grp + p[h * GQ:(h + 1) * GQ]
        p_pad = jnp.concatenate([p_grp, jnp.zeros((LANES - GQ, p_grp.shape[1]), _F32)], axis=0)
        p_hi, p_lo = _split_hi_lo(p_pad)
        p_slc = _dot_nt(c_ref[...], p_hi) + _dot_nt(c_ref[...], p_lo)

        blk = lax.broadcasted_iota(jnp.int32, (n_blk, 1), 0)
        blk_f = blk.astype(_F32)
        cur = jnp.zeros((1, LANES), jnp.int32)
        qi = lax.broadcasted_iota(jnp.int32, (1, LANES), 1) % (GQ // G)
        for j, cb in enumerate(cur_blocks):
            cur = jnp.where(qi == j, cb, cur)
        forced = (blk == 0) | (blk == cur) | (blk == cur - 1)
        score = jnp.where(forced, FORCE_SCORE, jnp.where(blk <= cur, p_slc, NEG))
        score = jnp.where(blk < n_sel, score, -jnp.inf)
        sel = jnp.zeros((n_blk, LANES), _F32)
        for _ in range(min(SEL_TOP, n_sel)):
            best = jnp.max(score, axis=0, keepdims=True)
            first = jnp.min(jnp.where(score == best, blk_f, float(n_blk)), axis=0, keepdims=True)
            hit = blk_f == first
            sel = jnp.where(hit, 1.0, sel)
            score = jnp.where(hit, -jnp.inf, score)
        sel_scr[...] = sel

        new = new_ref[0]
        pad = jnp.zeros((PAGE_SIZE - new.shape[0], KV_WIDTH), _F32)
        col = lambda j: _bf(jnp.concatenate([new[:, j * KV_WIDTH:(j + 1) * KV_WIDTH], pad], axis=0))
        ks_new, vs_new, kw_new, vw_new = col(2), col(3), col(4), col(5)

        p_w, p_wn = _softmax_rows([_dot(q, _bf(win_ref[0, 0])) + bwin_ref[...],
                                   _dot_nt(q, kw_new) + bnew_ref[...]])
        ow_scr[...] = _dot_nt(_bf(p_w), _bf(win_ref[0, 1])) + _dot(_bf(p_wn), vw_new)

        s_new = _dot_nt(q, ks_new) + bnew_ref[...]
        m0 = s_new.max(axis=1, keepdims=True)
        p_new = jnp.where(s_new > MASKED_BELOW, jnp.exp2(s_new - m0), 0.0)
        m_scr[...] = m0
        l_scr[...] = p_new.sum(axis=1, keepdims=True)
        acc_scr[...] = _dot(_bf(p_new), vs_new)

    keys_t = jnp.concatenate([_bf(r[0, 0]) for r in pages], axis=1)
    vals_t = jnp.concatenate([_bf(r[0, 1]) for r in pages], axis=1)
    bfar = bfar_ref[...]
    is_last = st == n_st - 1
    bias = jnp.concatenate([bfar] * (pps - 1) + [jnp.where(is_last, blast_ref[...], bfar)], axis=1)
    s = _dot(q, keys_t) + bias
    blk_per_step = nk // SEL_BLOCK
    sel_st = _bf(sel_scr[pl.ds(pl.multiple_of(st * blk_per_step, SUBLANES), blk_per_step), :])
    mask_gq = _dot_tn(sel_st, e_ref[...])[0:GQ]
    mask = jnp.concatenate([mask_gq] * HPG, axis=0) > 0.5
    s = jnp.where(mask, s, NEG)
    m_new = jnp.maximum(m_scr[...], s.max(axis=1, keepdims=True))
    alpha = jnp.exp2(m_scr[...] - m_new)
    p = jnp.where(mask, jnp.exp2(s - m_new), 0.0)
    l_scr[...] = alpha * l_scr[...] + p.sum(axis=1, keepdims=True)
    acc_scr[...] = alpha * acc_scr[...] + _dot_nt(_bf(p), vals_t)
    m_scr[...] = m_new

    @pl.when(is_last)
    def _():
        gt = jax.nn.sigmoid(gate_ref[0])
        l_fin = l_scr[...]
        o_sel = acc_scr[...] / jnp.where(l_fin > 0.0, l_fin, 1.0)
        o_ref[0] = gt[:, 0:1] * oc_scr[...] + gt[:, 1:2] * o_sel + gt[:, 2:3] * ow_scr[...]


def _nsa_sample(page_flat, cache, q32, gate32, parts, win, new8, b1k, b1v, w2k, w2v,
                bcmp, bwin, blast, bnew, bfar, cmat, n_cmp, n_sel, cur_blocks):
    db, R, _ = q32.shape
    n_pages = page_flat.shape[0] // db
    pps = min(PAGES_PER_STEP, n_pages)
    n_st = n_pages // pps
    page_spec = lambda j: pl.BlockSpec((1,) + cache.shape[1:],
                                       lambda b, s, pt: (pt[b * n_pages + s * pps + j], 0, 0, 0))
    per_seq = lambda a: pl.BlockSpec((1,) + a.shape[1:], lambda b, s, pt: (b,) + (0,) * (a.ndim - 1))
    const = lambda a: pl.BlockSpec(a.shape, lambda b, s, pt: (0,) * a.ndim)
    kern = functools.partial(_nsa_sample_kernel, n_cmp=n_cmp, n_sel=n_sel, cur_blocks=cur_blocks)
    nk = pps * PAGE_SIZE
    expand = jnp.asarray(np.arange(nk // SEL_BLOCK)[:, None] == np.arange(nk)[None, :] // SEL_BLOCK, _BF16)
    return pl.pallas_call(
        kern,
        grid_spec=pltpu.PrefetchScalarGridSpec(
            num_scalar_prefetch=1,
            grid=(db, n_st),
            in_specs=[page_spec(j) for j in range(pps)]
            + [per_seq(q32), per_seq(gate32), per_seq(parts), per_seq(win), per_seq(new8),
               const(b1k), const(b1v), const(w2k), const(w2v),
               const(bcmp), const(bwin), const(blast), const(bnew), const(bfar), const(cmat), const(expand)],
            out_specs=pl.BlockSpec((1, R, KV_WIDTH), lambda b, s, pt: (b, 0, 0)),
            scratch_shapes=[pltpu.VMEM((cmat.shape[0], LANES), _F32),
                            pltpu.VMEM((R, KV_WIDTH), _F32), pltpu.VMEM((R, KV_WIDTH), _F32),
                            pltpu.VMEM((R, 1), _F32), pltpu.VMEM((R, 1), _F32), pltpu.VMEM((R, KV_WIDTH), _F32)],
        ),
        out_shape=jax.ShapeDtypeStruct((db, R, KV_WIDTH), _F32),
        compiler_params=_params(("arbitrary", "arbitrary")),
        name="nsa_sample",
    )(page_flat, *([cache] * pps), q32, gate32, parts, win, new8, b1k, b1v, w2k, w2v,
      bcmp, bwin, blast, bnew, bfar, cmat, expand)


def _sb_sample_kernel(pt_ref, *refs):
    pps = len(refs) - 7
    pages = refs[:pps]
    q_ref, new_ref, nmask_ref, tt_ref, o_ref, acc_scr, car_scr = refs[pps:]
    st = pl.program_id(1)
    q = q_ref[0]
    tt = tt_ref[...]

    @pl.when(st == 0)
    def _():
        new = new_ref[0]
        pad = jnp.zeros((PAGE_SIZE - new.shape[0], SB_WIDTH), _F32)
        keys = _bf(jnp.concatenate([new[:, :SB_WIDTH], pad], axis=0))
        vals = _bf(jnp.concatenate([new[:, SB_WIDTH:], pad], axis=0))
        (a,), (carry,) = _sb_chunks([_dot_nt(q, keys)], [jnp.zeros(car_scr.shape, _F32)], tt,
                                    nmask_ref[...] > 0.5, True)
        acc_scr[...] = _dot(_bf(a), vals)
        car_scr[...] = carry

    keys_t = jnp.concatenate([_bf(r[0, 0]) for r in pages], axis=1)
    vals_t = jnp.concatenate([_bf(r[0, 1]) for r in pages], axis=1)
    (a,), (carry,) = _sb_chunks([_dot(q, keys_t)], [car_scr[...]], tt, None, True)
    acc_scr[...] = acc_scr[...] + _dot_nt(_bf(a), vals_t)
    car_scr[...] = carry

    @pl.when(st == pl.num_programs(1) - 1)
    def _():
        o_ref[0] = acc_scr[...]


def _sb_sample(page_flat, cache, q32, new8, nmask, tt):
    db, R, _ = q32.shape
    n_pages = page_flat.shape[0] // db
    pps = min(PAGES_PER_STEP, n_pages)
    n_st = n_pages // pps
    page_spec = lambda j: pl.BlockSpec(
        (1,) + cache.shape[1:], lambda b, s, pt: (pt[b * n_pages + n_pages - 1 - (s * pps + j)], 0, 0, 0))
    per_seq = lambda a: pl.BlockSpec((1,) + a.shape[1:], lambda b, s, pt: (b,) + (0,) * (a.ndim - 1))
    const = lambda a: pl.BlockSpec(a.shape, lambda b, s, pt: (0,) * a.ndim)
    return pl.pallas_call(
        _sb_sample_kernel,
        grid_spec=pltpu.PrefetchScalarGridSpec(
            num_scalar_prefetch=1,
            grid=(db, n_st),
            in_specs=[page_spec(j) for j in range(pps)] + [per_seq(q32), per_seq(new8), const(nmask), const(tt)],
            out_specs=pl.BlockSpec((1, R, SB_WIDTH), lambda b, s, pt: (b, 0, 0)),
            scratch_shapes=[pltpu.VMEM((R, SB_WIDTH), _F32), pltpu.VMEM((R, LANES), _F32)],
        ),
        out_shape=jax.ShapeDtypeStruct((db, R, SB_WIDTH), _F32),
        compiler_params=_params(("arbitrary", "arbitrary")),
        name="sb_sample",
    )(page_flat, *([cache] * pps), q32, new8, nmask, tt)


def _split_w_in(w_in):
    sizes = [NSA_WIDTH, 6 * KV_WIDTH, 3 * NSA_HEADS, SB_WIDTH, 2 * SB_WIDTH, 2 * D_MODEL]
    cuts = np.cumsum(sizes)[:-1].tolist()
    wqa, wkv, wgn, wqb, wsb, wgm = jnp.split(w_in, cuts, axis=1)
    wgn = jnp.pad(wgn, ((0, 0), (0, LANES - wgn.shape[1])))
    return [_bf(w) for w in (wqa, wkv, wgn, wqb, wsb, wgm)]


def _bucket_starts():
    b = _bucket_np(np.arange(4 * MAX_DISTANCE))
    assert b.max() == N_BUCKETS - 1 and (np.diff(b) >= 0).all()
    return [int(np.argmax(b >= k)) for k in range(1, N_BUCKETS)]


def _bias_of(dist, rb_ref, head, starts):
    out = jnp.full(dist.shape, rb_ref[N_BUCKETS - 1, head] * LOG2E, _F32)
    for b in range(N_BUCKETS - 2, -1, -1):
        out = jnp.where(dist < starts[b], rb_ref[b, head] * LOG2E, out)
    return out


def _bias_tables_kernel(rb_ref, bc_o, bs_o, bw_o, bfar_o, bfl_o, scmp_o, swin_o, slast_o, snew_o, sfar_o, *,
                        n_cmp_s):
    G, HPG, QT, Q = NSA_KV_GROUPS, HEADS_PER_GROUP, QUERY_TILE, DEC_SEQ
    starts = _bucket_starts()

    qi = lax.broadcasted_iota(jnp.int32, (1, QT), 1)

    def prompt_table(o_ref, dist_fn, valid_fn):
        r = lax.broadcasted_iota(jnp.int32, (o_ref.shape[1], 1), 0)
        dist = dist_fn(r, qi)
        valid = valid_fn(dist)
        for g in range(G):
            for h in range(HPG):
                o_ref[g, :, h * QT:(h + 1) * QT] = jnp.where(valid, _bias_of(dist, rb_ref, g * HPG + h, starts), NEG)

    prompt_table(bc_o, lambda r, i: i - CMP_STRIDE * (r - CMP_PAD) - (CMP_BLOCK - 1), lambda d: d >= 0)
    prompt_table(bs_o, lambda r, i: i + KEY_TILE - r, lambda d: d >= 0)
    prompt_table(bw_o, lambda r, i: i + WINDOW - r, lambda d: (d >= 0) & (d < WINDOW))
    for g in range(G):
        for h in range(HPG):
            bfar_o[g, :, h * QT:(h + 1) * QT] = jnp.full((1, QT), rb_ref[N_BUCKETS - 1, g * HPG + h] * LOG2E, _F32)
    nb = SEL_KEY_CHUNK // SEL_BLOCK
    lane = lax.broadcasted_iota(jnp.int32, (1, LANES), 1)
    bfl_o[...] = jnp.zeros(bfl_o.shape, _F32)
    for g in range(G):
        for h in range(HPG):
            far = jnp.full((1, LANES), rb_ref[N_BUCKETS - 1, g * HPG + h] * LOG2E, _F32)
            hi, lo = _split_hi_lo(far)
            bfl_o[g, h:h + 1, :] = jnp.where(lane == nb, hi.astype(_F32), jnp.where(lane == nb + 1, lo.astype(_F32), 0.0))

    GQ = G * Q
    row = lax.broadcasted_iota(jnp.int32, (GQ, 1), 0)
    row_g = row // Q
    t = PAST_LEN + row % Q

    def by_group(fn):
        out = fn(0)
        for g in range(1, G):
            out = jnp.where(row_g == g, fn(g), out)
        return out

    def sample_table(o_ref, kpos_fn, valid_fn):
        k = lax.broadcasted_iota(jnp.int32, (1, o_ref.shape[1]), 1)
        dist = t - kpos_fn(k)
        valid = valid_fn(dist, k)
        for h in range(HPG):
            b = by_group(lambda g: _bias_of(dist, rb_ref, g * HPG + h, starts))
            o_ref[h * GQ:(h + 1) * GQ, :] = jnp.where(valid, b, NEG)

    w_buf = swin_o.shape[1]
    sample_table(scmp_o, lambda k: CMP_STRIDE * k + CMP_BLOCK - 1, lambda d, k: (d >= 0) & (k < n_cmp_s))
    sample_table(swin_o, lambda k: PAST_LEN - w_buf + k,
                 lambda d, k: (d >= 0) & (d < WINDOW) & (PAST_LEN - w_buf + k >= 0))
    sample_table(slast_o, lambda k: PAST_LEN - PAGE_SIZE + k, lambda d, k: d >= 0)
    sample_table(snew_o, lambda k: PAST_LEN + k, lambda d, k: (d >= 0) & (k < Q))
    for h in range(HPG):
        far = by_group(lambda g: jnp.full((GQ, LANES), rb_ref[N_BUCKETS - 1, g * HPG + h] * LOG2E, _F32))
        sfar_o[h * GQ:(h + 1) * GQ, :] = far


def _bias_tables(rel_bias, n_chunk_s, n_cmp_s, w_buf):
    G, HPG, QT, Q = NSA_KV_GROUPS, HEADS_PER_GROUP, QUERY_TILE, DEC_SEQ
    NL, R = HPG * QT, HPG * G * Q
    f32 = lambda *s: jax.ShapeDtypeStruct(s, _F32)
    return pl.pallas_call(
        functools.partial(_bias_tables_kernel, n_cmp_s=n_cmp_s),
        in_specs=[pl.BlockSpec(memory_space=pltpu.SMEM)],
        out_shape=[f32(G, CMP_WIN, NL), f32(G, 2 * KEY_TILE, NL), f32(G, WINDOW + QT, NL), f32(G, 1, NL),
                   f32(G, SUBLANES, LANES), f32(R, n_chunk_s), f32(R, w_buf), f32(R, PAGE_SIZE), f32(R, PAGE_SIZE), f32(R, LANES)],
        compiler_params=_params(),
        name="bias_tables",
    )(rel_bias)


def kernel(x_prompt, x_sample, cache_cmp, cache_sel, cache_sb, state_win, state_conv, page_table,
           rel_bias, g_pre_mix, w_in, w_ck1, b_ck1, w_ck2, w_cv1, b_cv1, w_cv2, w_branch_a, w_branch_b,
           w_out, g_post_mix, g_pre_ffn, w_ffn_up, conv_w, conv_b, w_ffn_down, g_post_ffn):
    assert DEPTH == 1 and x_prompt.shape[0] == 1
    G, HPG, dh, Q = NSA_KV_GROUPS, HEADS_PER_GROUP, HEAD_DIM, DEC_SEQ
    T = x_prompt.shape[1]
    DB = x_sample.shape[0]
    n_pages = page_table.shape[1]
    w_buf = state_win.shape[2]
    row = lambda v: v.reshape(1, -1)

    w_proj = _split_w_in(w_in[0])
    g_pre = row(g_pre_mix[0])
    w1k, w1v = _bf(_chunk_weights(w_ck1[0])), _bf(_chunk_weights(w_cv1[0]))
    b1k, b1v = row(jnp.tile(b_ck1[0], G)), row(jnp.tile(b_cv1[0], G))
    w2k, w2v = _bf(_group_diag(w_ck2[0])), _bf(_group_diag(w_cv2[0]))
    wa, wb, wo = _bf(w_branch_a[0]), _bf(w_branch_b[0]), _bf(w_out[0])
    wua, wub = _bf(w_ffn_up[0][:, :D_FF]), _bf(w_ffn_up[0][:, D_FF:])
    wd = _bf(w_ffn_down[0])
    cw, cb = conv_w[0], row(conv_b[0])
    g_post_m, g_pre_f, g_post_f = row(g_post_mix[0]), row(g_pre_ffn[0]), row(g_post_ffn[0])
    tt_prompt = jnp.asarray(_cumsum_matrix(latest_first=False), _BF16)
    tt_sample = jnp.asarray(_cumsum_matrix(latest_first=True), _BF16)

    xp = x_prompt[0]
    qa, kv, kvb, gn, qb, sb, sbb, gm = _proj_in(xp, g_pre, w_proj)
    n_chunk = T // CMP_STRIDE
    n_cmp = (T - CMP_BLOCK) // CMP_STRIDE + 1
    kvcol = lambda a, j: a[:, j * KV_WIDTH:(j + 1) * KV_WIDTH]
    kc, vc = _compress_prompt(kvcol(kvb, 0).reshape(n_chunk, CMP_STRIDE * KV_WIDTH),
                              kvcol(kvb, 1).reshape(n_chunk, CMP_STRIDE * KV_WIDTH),
                              w1k, w1v, b1k, b1v, w2k, w2v)
    kcp = jnp.pad(kc, ((CMP_PAD, 0), (0, 0)))
    vcp = jnp.pad(vc, ((CMP_PAD, 0), (0, 0)))
    front = lambda a: jnp.pad(a, ((WINDOW, 0), (0, 0)))
    cpp = PAGE_SIZE // CMP_STRIDE
    n_chunk_s = n_pages * cpp
    n_cmp_s = (PAST_LEN + Q - CMP_BLOCK) // CMP_STRIDE + 1
    bc, bs, bw, bfar, bfl, bcmp, bwin, blast, bnew, bfar_s = _bias_tables(rel_bias, n_chunk_s, n_cmp_s, w_buf)
    ct =jnp.asarray(_cover_matrix(n_chunk + CMP_PAD, T // SEL_BLOCK, n_cmp, CMP_PAD).T, _BF16)
    o_a = _nsa_prompt(qa, gn, kcp, vcp, front(kvcol(kvb, 2)), front(kvcol(kvb, 3)),
                      front(kvcol(kvb, 4)), front(kvcol(kvb, 5)), bc, bs, bw, bfar, bfl, ct)
    o_b = _sb_prompt(qb, sbb, tt_prompt)
    x1, h2 = _merge(xp, o_a, o_b, gm, wa, wb, wo, g_post_m, g_pre_f)
    y_p, tails = _ffn_prompt(h2, x1, wua, wub, cw, cb, wd, g_post_f)

    kv_cmp_p = kv[:, 0:2 * KV_WIDTH].reshape(1, 1, T, 2, G, dh)
    kv_sel_p = kv[:, 2 * KV_WIDTH:4 * KV_WIDTH].reshape(1, 1, T, 2, G, dh)
    kv_sb_p = sb.reshape(1, 1, T, 2, SB_HEADS, dh)
    win_rows = jnp.pad(kv[:, 4 * KV_WIDTH:], ((WINDOW, 0), (0, 0)))[T + WINDOW - w_buf:]
    win_p = win_rows.reshape(1, 1, w_buf, 2, G, dh)
    conv_p = tails[-(CONV_W - 1):].reshape(1, 1, CONV_W - 1, D_FF)

    xs = jnp.transpose(x_sample, (1, 0, 2)).reshape(Q * DB, D_MODEL)
    qa_s, kv_s, _, gn_s, qb_s, sb_s, _, gm_s = _proj_in(xs, g_pre, w_proj)
    by_seq = lambda a: jnp.transpose(a.reshape(Q, DB, -1), (1, 0, 2))
    page_flat = page_table.reshape(-1)

    assert n_cmp_s < n_chunk_s + 1 and Q <= SUBLANES and Q <= CMP_STRIDE
    parts = _cmp_pages(page_flat, _pages_last(cache_cmp[0]), _bf(_page_chunk_weights(w_ck1[0], w_cv1[0])))
    parts = parts.reshape(DB, n_chunk_s, 4 * KV_WIDTH)

    q5 = by_seq(qa_s).reshape(DB, Q, G, HPG, dh)
    q5 = jnp.transpose(q5, (0, 3, 2, 1, 4))
    q32 = jnp.einsum('bhgqd,gk->bhgqkd', q5, jnp.eye(G, dtype=q5.dtype)).reshape(DB, HPG * G * Q, KV_WIDTH)
    g5 = by_seq(gn_s)[:, :, :3 * NSA_HEADS].reshape(DB, Q, 3, G, HPG)
    gate32 = jnp.transpose(g5, (0, 4, 3, 1, 2)).reshape(DB, HPG * G * Q, 3)
    gate32 = jnp.pad(gate32, ((0, 0), (0, 0), (0, LANES - 3)))
    new8 = jnp.pad(by_seq(kv_s), ((0, 0), (0, SUBLANES - Q), (0, 0)))
    n_sel_s = -(-(PAST_LEN + Q) // SEL_BLOCK)
    n_blk_s = -(-n_sel_s // LANES) * LANES
    cmat =jnp.asarray(_cover_matrix(n_chunk_s, n_blk_s, n_cmp_s, 0), _BF16)
    cmat = (cmat * (np.arange(n_blk_s)[None, :] < n_sel_s)).T
    cur_blocks = tuple(int((PAST_LEN + j) // SEL_BLOCK) for j in range(Q))
    o32 = _nsa_sample(page_flat, _pages_last(cache_sel[0]), q32, gate32, parts,
                      _pages_last(state_win[0]), new8, b1k, b1v, w2k, w2v,
                      bcmp, bwin, blast, bnew, bfar_s, cmat, n_cmp_s, n_sel_s, cur_blocks)
    o6 = o32.reshape(DB, HPG, G, Q, G, dh)
    o_a_s = jnp.stack([o6[:, :, g, :, g, :] for g in range(G)], axis=1)
    o_a_s = jnp.transpose(o_a_s, (3, 0, 1, 2, 4)).reshape(Q * DB, NSA_WIDTH)

    qb5 = jnp.transpose(by_seq(qb_s).reshape(DB, Q, SB_HEADS, dh), (0, 2, 1, 3))
    qsb = jnp.einsum('bhqd,hk->bhqkd', qb5, jnp.eye(SB_HEADS, dtype=qb5.dtype)).reshape(DB, SB_HEADS * Q, SB_WIDTH)
    sb_new8 = jnp.pad(by_seq(sb_s), ((0, 0), (0, SUBLANES - Q), (0, 0)))
    rq = np.arange(SB_HEADS * Q)[:, None] % Q
    nmask = jnp.asarray((np.arange(PAGE_SIZE)[None, :] < rq).astype(np.float32))
    osb = _sb_sample(page_flat, _pages_last(cache_sb[0]), qsb, sb_new8, nmask, tt_sample)
    o7 = osb.reshape(DB, SB_HEADS, Q, SB_HEADS, dh)
    o_b_s = jnp.stack([o7[:, h, :, h, :] for h in range(SB_HEADS)], axis=1)
    o_b_s = jnp.transpose(o_b_s, (2, 0, 1, 3)).reshape(Q * DB, SB_WIDTH)

    x1_s, h2_s = _merge(xs, _bf(o_a_s), _bf(o_b_s), gm_s, wa, wb, wo, g_post_m, g_pre_f)
    sc = jnp.transpose(state_conv[0], (1, 0, 2))
    y_s_rows, tail_s = _ffn_sample(h2_s, x1_s, sc, wua, wub, cw, cb, wd, g_post_f)

    y_s = by_seq(y_s_rows)
    kv_seq = by_seq(kv_s)
    kv_cmp_s = kv_seq[:, :, 0:2 * KV_WIDTH].reshape(1, DB, Q, 2, G, dh)
    kv_sel_s = kv_seq[:, :, 2 * KV_WIDTH:4 * KV_WIDTH].reshape(1, DB, Q, 2, G, dh)
    kv_sb_s = by_seq(sb_s).reshape(1, DB, Q, 2, SB_HEADS, dh)
    win_new = kv_seq[:, :, 4 * KV_WIDTH:].reshape(DB, Q, 2, G, dh)
    win_s = jnp.concatenate([state_win[0], win_new], axis=1)[:, -w_buf:][None]
    conv_s = jnp.transpose(tail_s.reshape(CONV_W - 1, DB, D_FF), (1, 0, 2))[None]

    return (y_p[None], y_s, kv_cmp_p, kv_sel_p, kv_sb_p, win_p, conv_p,
            kv_cmp_s, kv_sel_s, kv_sb_s, win_s, conv_s)
```

```python
import functools
import math

import numpy as np
import jax
import jax.numpy as jnp
from jax import lax
from jax.experimental import pallas as pl
from jax.experimental.pallas import tpu as pltpu

D_MODEL = 1024
SEQ = 16384
DEPTH = 1
DEC_BATCH = 128
DEC_SEQ = 4
PAST_LEN = 8192
PAGE_SIZE = 128
HEAD_DIM = 64
NSA_HEADS = 8
NSA_KV_GROUPS = 2
HEADS_PER_GROUP = NSA_HEADS // NSA_KV_GROUPS
SB_HEADS = 8
NSA_WIDTH = NSA_HEADS * HEAD_DIM
SB_WIDTH = SB_HEADS * HEAD_DIM
KV_WIDTH = NSA_KV_GROUPS * HEAD_DIM
CMP_BLOCK = 32
CMP_STRIDE = 16
SEL_BLOCK = 64
SEL_TOP = 16
SEL_COVER_W = (0.5, 1.0, 1.0, 1.0, 0.5)
WINDOW = 512
N_BUCKETS = 32
MAX_DISTANCE = 128
D_FF = 2816
CONV_W = 3
SCALE = HEAD_DIM ** -0.5
LOG2E = math.log2(math.e)
EPS = 1e-6
NEG = -1e30
FORCE_SCORE = 1e9
MASKED_BELOW = -5e29
MASK_PENALTY = -NEG
LANES = 128
SUBLANES = 8
QUERY_TILE = 128
KEY_TILE = 128
ROW_TILE = 256
CMP_PAD = 16
CMP_WIN = 24
PAGES_PER_STEP = 16
SB_QUERY_TILE = 256
SB_KEY_CHUNK = 2048
SEL_KEY_CHUNK = 512
VMEM_LIMIT = 56 * 1024 * 1024

_F32 = jnp.float32
_BF16 = jnp.bfloat16


def _bf(x):
    return x.astype(_BF16)


def _dot(a, b):
    return jnp.dot(a, b, preferred_element_type=_F32)


def _dot_nt(a, b):
    return lax.dot_general(a, b, (((1,), (1,)), ((), ())), preferred_element_type=_F32)


def _dot_tn(a, b):
    return lax.dot_general(a, b, (((0,), (0,)), ((), ())), preferred_element_type=_F32)


def _split_hi_lo(x):
    hi = _bf(x)
    lo = _bf(x - hi.astype(_F32))
    return hi, lo


def _rms(x, g):
    return x * lax.rsqrt(jnp.mean(x * x, axis=-1, keepdims=True) + EPS) * g


def _softplus_log2(z2):
    return jnp.maximum(z2, 0.0) + jnp.log2(1.0 + jnp.exp2(-jnp.abs(z2)))


def _const_spec(shape):
    nd = len(shape)
    return pl.BlockSpec(shape, lambda *_: (0,) * nd)


def _params(sem=None):
    return pltpu.CompilerParams(dimension_semantics=sem, vmem_limit_bytes=VMEM_LIMIT)


def _bucket_np(dist):
    n = np.maximum(dist, 0)
    max_exact = N_BUCKETS // 2
    nf = np.maximum(n, 1).astype(np.float32)
    large = max_exact + (np.log(nf / max_exact) / math.log(MAX_DISTANCE / max_exact)
                         * (N_BUCKETS - max_exact)).astype(np.int32)
    return np.where(n < max_exact, n, np.minimum(large, N_BUCKETS - 1)).astype(np.int32)


def _cover_matrix(n_rows, n_cols, n_cmp, row_offset):
    r = np.arange(n_rows)[:, None] - row_offset
    j = np.arange(n_cols)[None, :]
    k = r - 4 * j + 1
    w = np.asarray(SEL_COVER_W, np.float32)
    ok = (k >= 0) & (k <= 4) & (r >= 0) & (r < n_cmp)
    return np.where(ok, w[np.clip(k, 0, 4)], 0.0).astype(np.float32)


def _cumsum_matrix(latest_first):
    k = np.arange(2 * LANES)
    same = (k[:, None] // LANES) == (k[None, :] // LANES)
    within = same & (k[:, None] >= k[None, :])
    if latest_first:
        cross = (k[:, None] < LANES) & (k[None, :] >= LANES)
    else:
        cross = (k[:, None] >= LANES) & (k[None, :] < LANES)
    return (within | cross).astype(np.float32)


def _proj_in_kernel(x_ref, g_ref, wqa, wkv, wgn, wqb, wsb, wgm,
                    qa_o, kv_o, kvb_o, gn_o, qb_o, sb_o, sbb_o, gm_o):
    h = _bf(_rms(x_ref[...], g_ref[...]))
    qa_o[...] = _bf(_dot(h, wqa[...]) * (SCALE * LOG2E))
    kv = _dot(h, wkv[...])
    kv_o[...] = kv
    kvb_o[...] = _bf(kv)
    gn_o[...] = _dot(h, wgn[...])
    qb_o[...] = _bf(_dot(h, wqb[...]) * (SCALE * LOG2E))
    sb = _dot(h, wsb[...])
    sb_o[...] = sb
    sbb_o[...] = _bf(sb)
    gm_o[...] = _dot(h, wgm[...])


def _proj_in(x, g, ws):
    rows = x.shape[0]
    tm = min(ROW_TILE, rows)
    widths = [w.shape[1] for w in ws]
    out_shape = [
        jax.ShapeDtypeStruct((rows, widths[0]), _BF16),
        jax.ShapeDtypeStruct((rows, widths[1]), _F32),
        jax.ShapeDtypeStruct((rows, widths[1]), _BF16),
        jax.ShapeDtypeStruct((rows, widths[2]), _F32),
        jax.ShapeDtypeStruct((rows, widths[3]), _BF16),
        jax.ShapeDtypeStruct((rows, widths[4]), _F32),
        jax.ShapeDtypeStruct((rows, widths[4]), _BF16),
        jax.ShapeDtypeStruct((rows, widths[5]), _F32),
    ]
    row_spec = lambda w: pl.BlockSpec((tm, w), lambda i: (i, 0))
    return pl.pallas_call(
        _proj_in_kernel,
        grid=(rows // tm,),
        in_specs=[row_spec(D_MODEL), _const_spec((1, D_MODEL))] + [_const_spec(w.shape) for w in ws],
        out_specs=[row_spec(s.shape[1]) for s in out_shape],
        out_shape=out_shape,
        compiler_params=_params(("arbitrary",)),
        name="proj_in",
    )(x, g, *ws)


def _compress_finish(parts_lo, parts_hi, b1, w2):
    n = parts_hi.shape[0]
    hi_next = pltpu.roll(parts_hi, n - 1, 0)
    hid = parts_lo + hi_next + b1
    return _dot(_bf(jax.nn.gelu(hid)), w2)


def _compress_kernel(ck_ref, cv_ref, w1k, w1v, b1k, b1v, w2k, w2v, kc_o, vc_o):
    for c_ref, w1, b1, w2, o in ((ck_ref, w1k, b1k, w2k, kc_o), (cv_ref, w1v, b1v, w2v, vc_o)):
        parts = _dot(c_ref[...], w1[...])
        o[...] = _bf(_compress_finish(parts[:, :KV_WIDTH], parts[:, KV_WIDTH:], b1[...], w2[...]))


def _compress_prompt(ck, cv, w1k, w1v, b1k, b1v, w2k, w2v):
    n = ck.shape[0]
    return pl.pallas_call(
        _compress_kernel,
        out_shape=[jax.ShapeDtypeStruct((n, KV_WIDTH), _BF16)] * 2,
        compiler_params=_params(),
        name="compress_prompt",
    )(ck, cv, w1k, w1v, b1k, b1v, w2k, w2v)


def _chunk_weights(w1):
    G, dh = NSA_KV_GROUPS, HEAD_DIM
    w1r = w1.reshape(2, CMP_STRIDE, dh, dh)
    eye = jnp.eye(G, dtype=w1.dtype)
    big = jnp.einsum('slde,gh->lgdshe', w1r, eye)
    return big.reshape(CMP_STRIDE * G * dh, 2 * G * dh)


def _group_diag(w2):
    return jnp.kron(jnp.eye(NSA_KV_GROUPS, dtype=w2.dtype), w2)


def _nsa_prompt_kernel(qa_ref, gn_ref, kcp_ref, vcp_ref, ks_ref, vs_ref, kw_ref, vw_ref,
                       bc_ref, bs_ref, bw_ref, bfar_ref, bfl_ref, ct_ref, pat_ref, pick_ref,
                       oa_ref, s_scr, sel_scr, q_scr, pen_scr, sa_scr, sb_scr):
    G, HPG, dh = NSA_KV_GROUPS, HEADS_PER_GROUP, HEAD_DIM
    QT = QUERY_TILE
    NL = HPG * QT
    qb = pl.program_id(0)
    c = qb * QT
    n_cp = kcp_ref.shape[0]
    n_blk = ct_ref.shape[0]

    lane_half = lax.broadcasted_iota(jnp.int32, (1, LANES), 1) // dh
    gates_t = jnp.transpose(jax.nn.sigmoid(gn_ref[...]))

    def gate_row(branch, g):
        base = branch * NSA_HEADS + g * HPG
        return jnp.concatenate([gates_t[base + h:base + h + 1, :] for h in range(HPG)], axis=1)

    ti = c + lax.broadcasted_iota(jnp.int32, (1, QT), 1)
    cur = ti // SEL_BLOCK

    for g in range(G):
        q_rows = []
        for h in range(HPG):
            col = (g * HPG + h) * dh
            tile = qa_ref[:, (col // LANES) * LANES:(col // LANES + 1) * LANES].astype(_F32)
            if (col % LANES) // dh != g:
                tile = pltpu.roll(tile, dh, 1)
            q_rows.append(_bf(jnp.where(lane_half == g, tile, 0.0)))
        q_scr[g] = jnp.concatenate(q_rows, axis=0)

    o_cmp = [None] * G
    for g in range(G):
        qg = q_scr[g]
        bfar = bfar_ref[g]

        r0 = pl.multiple_of(qb * (QT // CMP_STRIDE), SUBLANES)
        s_scr[...] = _dot_nt(kcp_ref[...], qg) + bfar
        s_scr[pl.ds(r0, CMP_WIN), :] = s_scr[pl.ds(r0, CMP_WIN), :] - bfar + bc_ref[g]
        row = lax.broadcasted_iota(jnp.int32, (n_cp, 1), 0)
        s = jnp.where((row >= CMP_PAD) & (row < r0 + CMP_WIN), s_scr[...], NEG)
        m = jnp.max(s, axis=0, keepdims=True)
        e = jnp.exp2(s - m)
        p = e * jnp.where(m > MASKED_BELOW, 1.0 / jnp.sum(e, axis=0, keepdims=True), 0.0)
        o_cmp[g] = _dot_tn(vcp_ref[...], _bf(p))[g * dh:(g + 1) * dh]
        p_grp = p[:, 0:QT]
        for h in range(1, HPG):
            p_grp = p_grp + p[:, h * QT:(h + 1) * QT]
        p_hi, p_lo = _split_hi_lo(p_grp)
        p_slc = _dot(ct_ref[...], p_hi) + _dot(ct_ref[...], p_lo)

        blk = lax.broadcasted_iota(jnp.int32, (n_blk, 1), 0)
        blk_f = blk.astype(_F32)
        forced = (blk == 0) | (blk == cur) | (blk == cur - 1)
        score = jnp.where(forced, FORCE_SCORE, jnp.where(blk <= cur, p_slc, NEG))
        for _ in range(min(SEL_TOP, n_blk)):
            best = jnp.max(score, axis=0, keepdims=True)
            first = jnp.min(jnp.where(score == best, blk_f, float(n_blk)), axis=0, keepdims=True)
            score = jnp.where(blk_f == first, -jnp.inf, score)
        sel = jnp.where(score == -jnp.inf, 1.0, 0.0)
        sel_scr[g] = jnp.concatenate([sel] * HPG, axis=1)
        selm = _bf((jnp.transpose(sel) - 1.0) * MASK_PENALTY)
        pen_scr[g] = _dot(selm, pick_ref[...])

    def sel_rows(g, first_blk, n):
        rows = [jnp.broadcast_to(sel_scr[g, pl.ds(first_blk + j, 1), :], (SEL_BLOCK, NL)) for j in range(n)]
        return jnp.concatenate(rows, axis=0) > 0.5

    def sel_step(g, vals, s_tile, mask, carry, zero_masked):
        m_run, l_run, acc = carry
        if mask is not None:
            s_tile = jnp.where(mask, s_tile, NEG)
        m_new = jnp.maximum(m_run, jnp.max(s_tile, axis=0, keepdims=True))
        alpha = jnp.exp2(m_run - m_new)
        p_t = jnp.exp2(s_tile - m_new)
        if zero_masked:
            p_t = jnp.where(mask, p_t, 0.0)
        l_new = alpha * l_run + jnp.sum(p_t, axis=0, keepdims=True)
        pv = _dot_tn(vals, _bf(p_t))[g * dh:(g + 1) * dh]
        return m_new, l_new, alpha * acc + pv

    n_far_blk = jnp.maximum(qb - 1, 0) * (KEY_TILE // SEL_BLOCK)
    blk_per_chunk = SEL_KEY_CHUNK // SEL_BLOCK
    lane = lax.broadcasted_iota(jnp.int32, (1, LANES), 1)

    n_far_chunks = (n_far_blk + blk_per_chunk - 1) // blk_per_chunk

    def chunk_rows(f):
        return pl.multiple_of(f * SEL_KEY_CHUNK + WINDOW, KEY_TILE)

    def far_scores(f, out_scr):
        rows = chunk_rows(jnp.minimum(f, jnp.maximum(n_far_chunks - 1, 0)))
        keys = jnp.concatenate([ks_ref[pl.ds(rows, SEL_KEY_CHUNK), :], pat_ref[...]], axis=1)
        blk_of_lane = f * blk_per_chunk + lane
        off = jnp.where((lane < blk_per_chunk) & (blk_of_lane >= n_far_blk), -MASK_PENALTY, 0.0)
        col = pl.multiple_of(jnp.minimum(f, n_blk // blk_per_chunk - 1) * LANES, LANES)
        for g in range(G):
            pen = pen_scr[g, :, pl.ds(col, LANES)] + off
            extra = jnp.concatenate([_bf(pen + bfl_ref[g, h:h + 1, :]) for h in range(HPG)], axis=0)
            q_aug = jnp.concatenate([q_scr[g], extra], axis=1)
            out_scr[g] = _dot_nt(keys, q_aug)

    def far_consume(f, in_scr, state):
        vals = vs_ref[pl.ds(chunk_rows(jnp.minimum(f, jnp.maximum(n_far_chunks - 1, 0))), SEL_KEY_CHUNK), :]
        return tuple(sel_step(g, vals, in_scr[g], None, state[g], False) for g in range(G))

    def far_body(k, state):
        far_scores(2 * k + 1, sb_scr)
        state = far_consume(2 * k, sa_scr, state)
        far_scores(2 * k + 2, sa_scr)
        return far_consume(2 * k + 1, sb_scr, state)

    init = tuple((jnp.full((1, NL), NEG, _F32), jnp.zeros((1, NL), _F32), jnp.zeros((dh, NL), _F32))
                 for _ in range(G))
    far_scores(0, sa_scr)
    far = lax.fori_loop(0, (n_far_chunks + 1) // 2, far_body, init)

    for g in range(G):
        qg = q_scr[g]
        rn = pl.multiple_of(c + WINDOW - KEY_TILE, KEY_TILE)
        keys = ks_ref[pl.ds(rn, 2 * KEY_TILE), :]
        vals = vs_ref[pl.ds(rn, 2 * KEY_TILE), :]
        bias_near = bs_ref[g]
        s_tile = _dot_nt(keys, qg) + bias_near
        kpos = c - KEY_TILE + lax.broadcasted_iota(jnp.int32, (2 * KEY_TILE, 1), 0)
        first_blk = jnp.maximum(qb - 1, 0) * (KEY_TILE // SEL_BLOCK)
        near_sel = jnp.concatenate([
            jnp.where(qb > 0, 1.0, 0.0) * sel_rows(g, first_blk, KEY_TILE // SEL_BLOCK).astype(_F32),
            sel_rows(g, qb * (KEY_TILE // SEL_BLOCK), KEY_TILE // SEL_BLOCK).astype(_F32)], axis=0) > 0.5
        mask = near_sel & (bias_near > MASKED_BELOW) & (kpos >= 0)
        _, l_fin, acc = sel_step(g, vals, s_tile, mask, far[g], True)
        o_sel = acc * (1.0 / jnp.where(l_fin > 0.0, l_fin, 1.0))

        rw = pl.multiple_of(c, KEY_TILE)
        n_w = WINDOW + QT
        bias_w = bw_ref[g]
        s_w = _dot_nt(kw_ref[pl.ds(rw, n_w), :], qg) + bias_w
        kpos_w = c - WINDOW + lax.broadcasted_iota(jnp.int32, (n_w, 1), 0)
        mask_w = (bias_w > MASKED_BELOW) & (kpos_w >= 0)
        s_w = jnp.where(mask_w, s_w, NEG)
        m_w = jnp.max(s_w, axis=0, keepdims=True)
        e_w = jnp.exp2(s_w - m_w)
        p_w = e_w * jnp.where(m_w > MASKED_BELOW, 1.0 / jnp.sum(e_w, axis=0, keepdims=True), 0.0)
        o_win = _dot_tn(vw_ref[pl.ds(rw, n_w), :], _bf(p_w))[g * dh:(g + 1) * dh]

        o_t = gate_row(0, g) * o_cmp[g] + gate_row(1, g) * o_sel + gate_row(2, g) * o_win
        for pair in range(HPG // 2):
            blk_t = jnp.concatenate([o_t[:, (2 * pair) * QT:(2 * pair + 1) * QT],
                                     o_t[:, (2 * pair + 1) * QT:(2 * pair + 2) * QT]], axis=0)
            col = (g * HPG + 2 * pair) * dh
            oa_ref[:, col:col + LANES] = _bf(jnp.transpose(blk_t))


def _chunk_pattern():
    nb = SEL_KEY_CHUNK // SEL_BLOCK
    pat = np.zeros((SEL_KEY_CHUNK, LANES), np.float32)
    pat[np.arange(SEL_KEY_CHUNK), np.arange(SEL_KEY_CHUNK) // SEL_BLOCK] = 1.0
    pat[:, nb:nb + 2] = 1.0
    return pat


def _nsa_prompt(qa, gn, kcp, vcp, ks, vs, kw, vw, bc, bs, bw, bfar, bfl, ct):
    T = qa.shape[0]
    QT = QUERY_TILE
    NL = HEADS_PER_GROUP * QT
    pat = jnp.asarray(_chunk_pattern(), _BF16)
    n_blk = ct.shape[0]
    nb = SEL_KEY_CHUNK // SEL_BLOCK
    b = np.arange(n_blk)
    pick = np.zeros((n_blk, n_blk // nb * LANES), np.float32)
    pick[b, b // nb * LANES + b % nb] = 1.0
    pick = jnp.asarray(pick, _BF16)
    full = lambda a: _const_spec(a.shape)
    return pl.pallas_call(
        _nsa_prompt_kernel,
        grid=(T // QT,),
        in_specs=[pl.BlockSpec((QT, NSA_WIDTH), lambda i: (i, 0)),
                  pl.BlockSpec((QT, LANES), lambda i: (i, 0)),
                  full(kcp), full(vcp), full(ks), full(vs), full(kw), full(vw),
                  full(bc), full(bs), full(bw), full(bfar), full(bfl), full(ct), full(pat), full(pick)],
        out_specs=pl.BlockSpec((QT, NSA_WIDTH), lambda i: (i, 0)),
        out_shape=jax.ShapeDtypeStruct((T, NSA_WIDTH), _BF16),
        scratch_shapes=[pltpu.VMEM((kcp.shape[0], NL), _F32),
                        pltpu.VMEM((NSA_KV_GROUPS, n_blk, NL), _F32),
                        pltpu.VMEM((NSA_KV_GROUPS, NL, LANES), _BF16),
                        pltpu.VMEM((NSA_KV_GROUPS, QT, pick.shape[1]), _F32),
                        pltpu.VMEM((NSA_KV_GROUPS, SEL_KEY_CHUNK, NL), _F32),
                        pltpu.VMEM((NSA_KV_GROUPS, SEL_KEY_CHUNK, NL), _F32)],
        compiler_params=_params(("arbitrary",)),
        name="nsa_prompt",
    )(qa, gn, kcp, vcp, ks, vs, kw, vw, bc, bs, bw, bfar, bfl, ct, pat, pick)


def _sb_chunks(zs, carries, tt, mask, latest_first):
    width = zs[0].shape[1]
    step = min(2 * LANES, width)
    tri = tt if step == 2 * LANES else tt[:LANES, :LANES]
    n = width // step
    sps = []
    for z in zs:
        sp = _softplus_log2(z)
        sps.append(sp if mask is None else jnp.where(mask, sp, 0.0))
    carries = list(carries)
    a_parts = [[None] * n for _ in zs]
    for j in (range(n) if latest_first else reversed(range(n))):
        sl = slice(j * step, (j + 1) * step)
        css = [_dot(_bf(sp[:, sl]), tri) for sp in sps]
        for i, z in enumerate(zs):
            later = jnp.concatenate([carries[i]] * (step // LANES), axis=1)
            a_parts[i][j] = jnp.exp2(z[:, sl] - css[i] - later)
            carries[i] = carries[i] + jnp.sum(sps[i][:, sl], axis=1, keepdims=True)
    out = []
    for pieces in a_parts:
        a = pieces[0] if n == 1 else jnp.concatenate(pieces, axis=1)
        out.append(a if mask is None else jnp.where(mask, a, 0.0))
    return out, carries


def _sb_prompt_kernel(q_ref, k_ref, v_ref, tt_ref, o_ref, acc_scr, car_scr):
    dh = HEAD_DIM
    QT = q_ref.shape[0]
    KW = min(SB_KEY_CHUNK, k_ref.shape[0])
    i = pl.program_id(1)
    lane_half = lax.broadcasted_iota(jnp.int32, (1, LANES), 1) // dh
    q = q_ref[...].astype(_F32)
    qh = [_bf(jnp.where(lane_half == h, q, 0.0)) for h in range(2)]
    tt = tt_ref[...]
    last = (i * QT + QT - 1) // KW

    def chunk(ci, mask, width=KW):
        r = pl.multiple_of(ci * KW, KW)
        keys = k_ref[pl.ds(r, width), :]
        vals = v_ref[pl.ds(r, width), :]
        zs = [_dot_nt(qh[h], keys) for h in range(2)]
        ws, carries = _sb_chunks(zs, [car_scr[h] for h in range(2)], tt, mask, latest_first=False)
        for h in range(2):
            acc_scr[h] = acc_scr[h] + _dot(_bf(ws[h]), vals)
            car_scr[h] = carries[h]

    acc_scr[...] = jnp.zeros(acc_scr.shape, _F32)
    car_scr[...] = jnp.zeros(car_scr.shape, _F32)
    t = i * QT + lax.broadcasted_iota(jnp.int32, (QT, 1), 0)
    tiles_per_chunk = KW // QT
    for w in range(1, tiles_per_chunk + 1):
        @pl.when(i % tiles_per_chunk == w - 1)
        def _(w=w):
            kpos = last * KW + lax.broadcasted_iota(jnp.int32, (1, w * QT), 1)
            chunk(last, kpos < t, w * QT)

    def body(n, _):
        chunk(last - 1 - n, None)
        return 0

    lax.fori_loop(0, last, body, 0)
    o_ref[...] = _bf(jnp.where(lane_half == 0, acc_scr[0], acc_scr[1]))


def _sb_prompt(qb, sbb, tt):
    T = qb.shape[0]
    QT = min(SB_QUERY_TILE, T)
    n_pair = SB_WIDTH // LANES
    return pl.pallas_call(
        _sb_prompt_kernel,
        grid=(n_pair, T // QT),
        in_specs=[pl.BlockSpec((QT, LANES), lambda p, i: (i, p)),
                  pl.BlockSpec((T, LANES), lambda p, i: (0, p)),
                  pl.BlockSpec((T, LANES), lambda p, i: (0, n_pair + p)),
                  _const_spec(tt.shape)],
        out_specs=pl.BlockSpec((QT, LANES), lambda p, i: (i, p)),
        out_shape=jax.ShapeDtypeStruct((T, SB_WIDTH), _BF16),
        scratch_shapes=[pltpu.VMEM((2, QT, LANES), _F32), pltpu.VMEM((2, QT, LANES), _F32)],
        compiler_params=_params(("arbitrary", "arbitrary")),
        name="sb_prompt",
    )(qb, sbb, sbb, tt)


def _merge_kernel(x_ref, oa_ref, ob_ref, gm_ref, wa, wb, wo, g_post, g_pre, x1_o, h2_o):
    gm = gm_ref[...]
    mixed = (jax.nn.sigmoid(gm[:, :D_MODEL]) * _dot(oa_ref[...], wa[...])
             + jax.nn.sigmoid(gm[:, D_MODEL:]) * _dot(ob_ref[...], wb[...]))
    x1 = x_ref[...] + _rms(_dot(_bf(mixed), wo[...]), g_post[...])
    x1_o[...] = x1
    h2_o[...] = _bf(_rms(x1, g_pre[...]))


def _merge(x, oa, ob, gm, wa, wb, wo, g_post, g_pre):
    rows = x.shape[0]
    tm = min(ROW_TILE, rows)
    row_spec = lambda w: pl.BlockSpec((tm, w), lambda i: (i, 0))
    return pl.pallas_call(
        _merge_kernel,
        grid=(rows // tm,),
        in_specs=[row_spec(D_MODEL), row_spec(NSA_WIDTH), row_spec(SB_WIDTH), row_spec(2 * D_MODEL),
                  _const_spec(wa.shape), _const_spec(wb.shape), _const_spec(wo.shape),
                  _const_spec((1, D_MODEL)), _const_spec((1, D_MODEL))],
        out_specs=[row_spec(D_MODEL), row_spec(D_MODEL)],
        out_shape=[jax.ShapeDtypeStruct((rows, D_MODEL), _F32), jax.ShapeDtypeStruct((rows, D_MODEL), _BF16)],
        compiler_params=_params(("arbitrary",)),
        name="merge",
    )(x, oa, ob, gm, wa, wb, wo, g_post, g_pre)


def _ffn_tail(x1, conv, gate, wd, g_post):
    f = jax.nn.gelu(conv) * gate
    return x1 + _rms(_dot(_bf(f), wd), g_post)


def _ffn_prompt_kernel(h2_ref, halo_ref, x1_ref, wua, wub, cw_ref, cb_ref, wd, g_post, y_o, tail_o):
    i = pl.program_id(0)
    tm = h2_ref.shape[0]
    h2 = h2_ref[...]
    a_ext = _dot(jnp.concatenate([halo_ref[...], h2], axis=0), wua[...])
    row = lax.broadcasted_iota(jnp.int32, (SUBLANES + tm, 1), 0)
    a_ext = jnp.where((row < SUBLANES) & (i == 0), 0.0, a_ext)
    a0 = a_ext[SUBLANES:]
    a1 = pltpu.roll(a_ext, 1, 0)[SUBLANES:]
    a2 = pltpu.roll(a_ext, 2, 0)[SUBLANES:]
    cw = cw_ref[...]
    conv = cb_ref[...] + cw[0:1] * a2 + cw[1:2] * a1 + cw[2:3] * a0
    y_o[...] = _ffn_tail(x1_ref[...], conv, _dot(h2, wub[...]), wd[...], g_post[...])
    tail_o[...] = a0[tm - SUBLANES:]


def _ffn_prompt(h2, x1, wua, wub, cw, cb, wd, g_post):
    T = h2.shape[0]
    tm = min(ROW_TILE, T)
    hb = tm // SUBLANES
    return pl.pallas_call(
        _ffn_prompt_kernel,
        grid=(T // tm,),
        in_specs=[pl.BlockSpec((tm, D_MODEL), lambda i: (i, 0)),
                  pl.BlockSpec((SUBLANES, D_MODEL), lambda i: (jnp.maximum(i * hb - 1, 0), 0)),
                  pl.BlockSpec((tm, D_MODEL), lambda i: (i, 0)),
                  _const_spec(wua.shape), _const_spec(wub.shape), _const_spec(cw.shape),
                  _const_spec(cb.shape), _const_spec(wd.shape), _const_spec((1, D_MODEL))],
        out_specs=[pl.BlockSpec((tm, D_MODEL), lambda i: (i, 0)),
                   pl.BlockSpec((SUBLANES, D_FF), lambda i: (i, 0))],
        out_shape=[jax.ShapeDtypeStruct((T, D_MODEL), _F32),
                   jax.ShapeDtypeStruct((T // tm * SUBLANES, D_FF), _F32)],
        compiler_params=_params(("arbitrary",)),
        name="ffn_prompt",
    )(h2, h2, x1, wua, wub, cw, cb, wd, g_post)


def _ffn_sample_kernel(h2_ref, x1_ref, sc_ref, wua, wub, cw_ref, cb_ref, wd, g_post, y_o, tail_o):
    db = sc_ref.shape[1]
    nq = h2_ref.shape[0] // db
    h2 = h2_ref[...]
    a0 = _dot(h2, wua[...])
    a1 = jnp.concatenate([sc_ref[1], a0[:(nq - 1) * db]], axis=0)
    a2 = jnp.concatenate([sc_ref[0], sc_ref[1], a0[:(nq - 2) * db]], axis=0)
    cw = cw_ref[...]
    conv = cb_ref[...] + cw[0:1] * a2 + cw[1:2] * a1 + cw[2:3] * a0
    y_o[...] = _ffn_tail(x1_ref[...], conv, _dot(h2, wub[...]), wd[...], g_post[...])
    tail_o[...] = a0[(nq - 2) * db:]


def _ffn_sample(h2, x1, sc, wua, wub, cw, cb, wd, g_post):
    rows = h2.shape[0]
    db = sc.shape[1]
    return pl.pallas_call(
        _ffn_sample_kernel,
        out_shape=[jax.ShapeDtypeStruct((rows, D_MODEL), _F32),
                   jax.ShapeDtypeStruct((2 * db, D_FF), _F32)],
        compiler_params=_params(),
        name="ffn_sample",
    )(h2, x1, sc, wua, wub, cw, cb, wd, g_post)


def _cmp_pages_kernel(pt_ref, *refs):
    n = len(refs) - 3
    w_ref, o_ref, x_scr = refs[n:]
    cpp = PAGE_SIZE // CMP_STRIDE
    pitch = x_scr.shape[1] // CMP_STRIDE
    for kv in range(2):
        for j, r in enumerate(refs[:n]):
            x = jnp.transpose(r[0, kv])
            for c in range(cpp):
                x_scr[kv, pl.ds(j * cpp + c, CMP_STRIDE, stride=pitch), :] = x[c * CMP_STRIDE:(c + 1) * CMP_STRIDE]
        acc = jnp.zeros((n * cpp, 2 * KV_WIDTH), _F32)
        for l in range(CMP_STRIDE):
            acc = acc + _dot(_bf(x_scr[kv, l * pitch:l * pitch + n * cpp, :]), w_ref[l, kv])
        o_ref[:, kv * 2 * KV_WIDTH:(kv + 1) * 2 * KV_WIDTH] = acc


def _slab_pitch(rows):
    tiles = -(-rows // SUBLANES)
    return SUBLANES * (tiles if tiles % 2 else tiles + 1)


def _cmp_pages(page_flat, cache_t, w):
    n_used = page_flat.shape[0]
    cpp = PAGE_SIZE // CMP_STRIDE
    pps = min(PAGES_PER_STEP, n_used)
    page_spec = lambda j: pl.BlockSpec((1,) + cache_t.shape[1:], lambda i, pt: (pt[i * pps + j], 0, 0, 0))
    return pl.pallas_call(
        _cmp_pages_kernel,
        grid_spec=pltpu.PrefetchScalarGridSpec(
            num_scalar_prefetch=1,
            grid=(n_used // pps,),
            in_specs=[page_spec(j) for j in range(pps)] + [pl.BlockSpec(w.shape, lambda i, pt: (0, 0, 0, 0))],
            out_specs=pl.BlockSpec((pps * cpp, 4 * KV_WIDTH), lambda i, pt: (i, 0)),
            scratch_shapes=[pltpu.VMEM((2, CMP_STRIDE * _slab_pitch(pps * cpp), KV_WIDTH), _F32)],
        ),
        out_shape=jax.ShapeDtypeStruct((n_used * cpp, 4 * KV_WIDTH), _F32),
        compiler_params=_params(("arbitrary",)),
        name="cmp_pages",
    )(page_flat, *([cache_t] * pps), w)


def _page_chunk_weights(w1k, w1v):
    G, dh = NSA_KV_GROUPS, HEAD_DIM
    w = jnp.stack([w1k.reshape(2, CMP_STRIDE, dh, dh), w1v.reshape(2, CMP_STRIDE, dh, dh)])
    big = jnp.einsum('ksLde,gh->Lkgdshe', w, jnp.eye(G, dtype=w.dtype))
    return big.reshape(CMP_STRIDE, 2, G * dh, 2 * G * dh)


def _pages_last(cache):
    n, p, kv, h, d = cache.shape
    return jnp.transpose(cache, (0, 2, 3, 4, 1)).reshape(n, kv, h * d, p)


def _softmax_rows(parts):
    m = parts[0].max(axis=1, keepdims=True)
    for s in parts[1:]:
        m = jnp.maximum(m, s.max(axis=1, keepdims=True))
    es = [jnp.exp2(s - m) for s in parts]
    l = es[0].sum(axis=1, keepdims=True)
    for e in es[1:]:
        l = l + e.sum(axis=1, keepdims=True)
    inv = jnp.where(m > MASKED_BELOW, 1.0 / l, 0.0)
    return [e * inv for e in es]


def _nsa_sample_kernel(pt_ref, *refs, n_cmp, n_sel, cur_blocks):
    pps = len(refs) - 23
    pages = refs[:pps]
    (q_ref, gate_ref, parts_ref, win_ref, new_ref, b1k, b1v, w2k, w2v,
     bcmp_ref, bwin_ref, blast_ref, bnew_ref, bfar_ref, c_ref, e_ref,
     o_ref, sel_scr, oc_scr, ow_scr, m_scr, l_scr, acc_scr) = refs[pps:]
    G, HPG, dh = NSA_KV_GROUPS, HEADS_PER_GROUP, HEAD_DIM
    R = q_ref.shape[1]
    GQ = R // HPG
    st = pl.program_id(1)
    n_st = pl.num_programs(1)
    q = q_ref[0]
    n_blk = c_ref.shape[0]
    nk = pps * PAGE_SIZE

    @pl.when(st == 0)
    def _():
        parts = parts_ref[0]
        kc = _bf(_compress_finish(parts[:, 0:KV_WIDTH], parts[:, KV_WIDTH:2 * KV_WIDTH], b1k[...], w2k[...]))
        vc = _bf(_compress_finish(parts[:, 2 * KV_WIDTH:3 * KV_WIDTH], parts[:, 3 * KV_WIDTH:], b1v[...], w2v[...]))
        (p,) = _softmax_rows([_dot_nt(q, kc) + bcmp_ref[...]])
        oc_scr[...] = _dot(_bf(p), vc)
        p_grp = p[0:GQ]
        for h in range(1, HPG):
            p_grp = p_grp + p[h * GQ:(h + 1) * GQ]
        p_pad = jnp.concatenate([p_grp, jnp.zeros((LANES - GQ, p_grp.shape[1]), _F32)], axis=0)
        p_hi, p_lo = _split_hi_lo(p_pad)
        p_slc = _dot_nt(c_ref[...], p_hi) + _dot_nt(c_ref[...], p_lo)

        blk = lax.broadcasted_iota(jnp.int32, (n_blk, 1), 0)
        blk_f = blk.astype(_F32)
        cur = jnp.zeros((1, LANES), jnp.int32)
        qi = lax.broadcasted_iota(jnp.int32, (1, LANES), 1) % (GQ // G)
        for j, cb in enumerate(cur_blocks):
            cur = jnp.where(qi == j, cb, cur)
        forced = (blk == 0) | (blk == cur) | (blk == cur - 1)
        score = jnp.where(forced, FORCE_SCORE, jnp.where(blk <= cur, p_slc, NEG))
        score = jnp.where(blk < n_sel, score, -jnp.inf)
        sel = jnp.zeros((n_blk, LANES), _F32)
        for _ in range(min(SEL_TOP, n_sel)):
            best = jnp.max(score, axis=0, keepdims=True)
            first = jnp.min(jnp.where(score == best, blk_f, float(n_blk)), axis=0, keepdims=True)
            hit = blk_f == first
            sel = jnp.where(hit, 1.0, sel)
            score = jnp.where(hit, -jnp.inf, score)
        sel_scr[...] = sel

        new = new_ref[0]
        pad = jnp.zeros((PAGE_SIZE - new.shape[0], KV_WIDTH), _F32)
        col = lambda j: _bf(jnp.concatenate([new[:, j * KV_WIDTH:(j + 1) * KV_WIDTH], pad], axis=0))
        ks_new, vs_new, kw_new, vw_new = col(2), col(3), col(4), col(5)

        p_w, p_wn = _softmax_rows([_dot(q, _bf(win_ref[0, 0])) + bwin_ref[...],
                                   _dot_nt(q, kw_new) + bnew_ref[...]])
        ow_scr[...] = _dot_nt(_bf(p_w), _bf(win_ref[0, 1])) + _dot(_bf(p_wn), vw_new)

        s_new = _dot_nt(q, ks_new) + bnew_ref[...]
        m0 = s_new.max(axis=1, keepdims=True)
        p_new = jnp.where(s_new > MASKED_BELOW, jnp.exp2(s_new - m0), 0.0)
        m_scr[...] = m0
        l_scr[...] = p_new.sum(axis=1, keepdims=True)
        acc_scr[...] = _dot(_bf(p_new), vs_new)

    keys_t = jnp.concatenate([_bf(r[0, 0]) for r in pages], axis=1)
    vals_t = jnp.concatenate([_bf(r[0, 1]) for r in pages], axis=1)
    bfar = bfar_ref[...]
    is_last = st == n_st - 1
    bias = jnp.concatenate([bfar] * (pps - 1) + [jnp.where(is_last, blast_ref[...], bfar)], axis=1)
    s = _dot(q, keys_t) + bias
    blk_per_step = nk // SEL_BLOCK
    sel_st = _bf(sel_scr[pl.ds(pl.multiple_of(st * blk_per_step, SUBLANES), blk_per_step), :])
    mask_gq = _dot_tn(sel_st, e_ref[...])[0:GQ]
    mask = jnp.concatenate([mask_gq] * HPG, axis=0) > 0.5
    s = jnp.where(mask, s, NEG)
    m_new = jnp.maximum(m_scr[...], s.max(axis=1, keepdims=True))
    alpha = jnp.exp2(m_scr[...] - m_new)
    p = jnp.where(mask, jnp.exp2(s - m_new), 0.0)
    l_scr[...] = alpha * l_scr[...] + p.sum(axis=1, keepdims=True)
    acc_scr[...] = alpha * acc_scr[...] + _dot_nt(_bf(p), vals_t)
    m_scr[...] = m_new

    @pl.when(is_last)
    def _():
        gt = jax.nn.sigmoid(gate_ref[0])
        l_fin = l_scr[...]
        o_sel = acc_scr[...] / jnp.where(l_fin > 0.0, l_fin, 1.0)
        o_ref[0] = gt[:, 0:1] * oc_scr[...] + gt[:, 1:2] * o_sel + gt[:, 2:3] * ow_scr[...]


def _nsa_sample(page_flat, cache, q32, gate32, parts, win, new8, b1k, b1v, w2k, w2v,
                bcmp, bwin, blast, bnew, bfar, cmat, n_cmp, n_sel, cur_blocks):
    db, R, _ = q32.shape
    n_pages = page_flat.shape[0] // db
    pps = min(PAGES_PER_STEP, n_pages)
    n_st = n_pages // pps
    page_spec = lambda j: pl.BlockSpec((1,) + cache.shape[1:],
                                       lambda b, s, pt: (pt[b * n_pages + s * pps + j], 0, 0, 0))
    per_seq = lambda a: pl.BlockSpec((1,) + a.shape[1:], lambda b, s, pt: (b,) + (0,) * (a.ndim - 1))
    const = lambda a: pl.BlockSpec(a.shape, lambda b, s, pt: (0,) * a.ndim)
    kern = functools.partial(_nsa_sample_kernel, n_cmp=n_cmp, n_sel=n_sel, cur_blocks=cur_blocks)
    nk = pps * PAGE_SIZE
    expand = jnp.asarray(np.arange(nk // SEL_BLOCK)[:, None] == np.arange(nk)[None, :] // SEL_BLOCK, _BF16)
    return pl.pallas_call(
        kern,
        grid_spec=pltpu.PrefetchScalarGridSpec(
            num_scalar_prefetch=1,
            grid=(db, n_st),
            in_specs=[page_spec(j) for j in range(pps)]
            + [per_seq(q32), per_seq(gate32), per_seq(parts), per_seq(win), per_seq(new8),
               const(b1k), const(b1v), const(w2k), const(w2v),
               const(bcmp), const(bwin), const(blast), const(bnew), const(bfar), const(cmat), const(expand)],
            out_specs=pl.BlockSpec((1, R, KV_WIDTH), lambda b, s, pt: (b, 0, 0)),
            scratch_shapes=[pltpu.VMEM((cmat.shape[0], LANES), _F32),
                            pltpu.VMEM((R, KV_WIDTH), _F32), pltpu.VMEM((R, KV_WIDTH), _F32),
                            pltpu.VMEM((R, 1), _F32), pltpu.VMEM((R, 1), _F32), pltpu.VMEM((R, KV_WIDTH), _F32)],
        ),
        out_shape=jax.ShapeDtypeStruct((db, R, KV_WIDTH), _F32),
        compiler_params=_params(("arbitrary", "arbitrary")),
        name="nsa_sample",
    )(page_flat, *([cache] * pps), q32, gate32, parts, win, new8, b1k, b1v, w2k, w2v,
      bcmp, bwin, blast, bnew, bfar, cmat, expand)


def _sb_sample_kernel(pt_ref, *refs):
    pps = len(refs) - 7
    pages = refs[:pps]
    q_ref, new_ref, nmask_ref, tt_ref, o_ref, acc_scr, car_scr = refs[pps:]
    st = pl.program_id(1)
    q = q_ref[0]
    tt = tt_ref[...]

    @pl.when(st == 0)
    def _():
        new = new_ref[0]
        pad = jnp.zeros((PAGE_SIZE - new.shape[0], SB_WIDTH), _F32)
        keys = _bf(jnp.concatenate([new[:, :SB_WIDTH], pad], axis=0))
        vals = _bf(jnp.concatenate([new[:, SB_WIDTH:], pad], axis=0))
        (a,), (carry,) = _sb_chunks([_dot_nt(q, keys)], [jnp.zeros(car_scr.shape, _F32)], tt,
                                    nmask_ref[...] > 0.5, True)
        acc_scr[...] = _dot(_bf(a), vals)
        car_scr[...] = carry

    keys_t = jnp.concatenate([_bf(r[0, 0]) for r in pages], axis=1)
    vals_t = jnp.concatenate([_bf(r[0, 1]) for r in pages], axis=1)
    (a,), (carry,) = _sb_chunks([_dot(q, keys_t)], [car_scr[...]], tt, None, True)
    acc_scr[...] = acc_scr[...] + _dot_nt(_bf(a), vals_t)
    car_scr[...] = carry

    @pl.when(st == pl.num_programs(1) - 1)
    def _():
        o_ref[0] = acc_scr[...]


def _sb_sample(page_flat, cache, q32, new8, nmask, tt):
    db, R, _ = q32.shape
    n_pages = page_flat.shape[0] // db
    pps = min(PAGES_PER_STEP, n_pages)
    n_st = n_pages // pps
    page_spec = lambda j: pl.BlockSpec(
        (1,) + cache.shape[1:], lambda b, s, pt: (pt[b * n_pages + n_pages - 1 - (s * pps + j)], 0, 0, 0))
    per_seq = lambda a: pl.BlockSpec((1,) + a.shape[1:], lambda b, s, pt: (b,) + (0,) * (a.ndim - 1))
    const = lambda a: pl.BlockSpec(a.shape, lambda b, s, pt: (0,) * a.ndim)
    return pl.pallas_call(
        _sb_sample_kernel,
        grid_spec=pltpu.PrefetchScalarGridSpec(
            num_scalar_prefetch=1,
            grid=(db, n_st),
            in_specs=[page_spec(j) for j in range(pps)] + [per_seq(q32), per_seq(new8), const(nmask), const(tt)],
            out_specs=pl.BlockSpec((1, R, SB_WIDTH), lambda b, s, pt: (b, 0, 0)),
            scratch_shapes=[pltpu.VMEM((R, SB_WIDTH), _F32), pltpu.VMEM((R, LANES), _F32)],
        ),
        out_shape=jax.ShapeDtypeStruct((db, R, SB_WIDTH), _F32),
        compiler_params=_params(("arbitrary", "arbitrary")),
        name="sb_sample",
    )(page_flat, *([cache] * pps), q32, new8, nmask, tt)


def _split_w_in(w_in):
    sizes = [NSA_WIDTH, 6 * KV_WIDTH, 3 * NSA_HEADS, SB_WIDTH, 2 * SB_WIDTH, 2 * D_MODEL]
    cuts = np.cumsum(sizes)[:-1].tolist()
    wqa, wkv, wgn, wqb, wsb, wgm = jnp.split(w_in, cuts, axis=1)
    wgn = jnp.pad(wgn, ((0, 0), (0, LANES - wgn.shape[1])))
    return [_bf(w) for w in (wqa, wkv, wgn, wqb, wsb, wgm)]


def _bucket_starts():
    b = _bucket_np(np.arange(4 * MAX_DISTANCE))
    assert b.max() == N_BUCKETS - 1 and (np.diff(b) >= 0).all()
    return [int(np.argmax(b >= k)) for k in range(1, N_BUCKETS)]


def _bias_of(dist, rb_ref, head, starts):
    out = jnp.full(dist.shape, rb_ref[N_BUCKETS - 1, head] * LOG2E, _F32)
    for b in range(N_BUCKETS - 2, -1, -1):
        out = jnp.where(dist < starts[b], rb_ref[b, head] * LOG2E, out)
    return out


def _bias_tables_kernel(rb_ref, bc_o, bs_o, bw_o, bfar_o, bfl_o, scmp_o, swin_o, slast_o, snew_o, sfar_o, *,
                        n_cmp_s):
    G, HPG, QT, Q = NSA_KV_GROUPS, HEADS_PER_GROUP, QUERY_TILE, DEC_SEQ
    starts = _bucket_starts()

    qi = lax.broadcasted_iota(jnp.int32, (1, QT), 1)

    def prompt_table(o_ref, dist_fn, valid_fn):
        r = lax.broadcasted_iota(jnp.int32, (o_ref.shape[1], 1), 0)
        dist = dist_fn(r, qi)
        valid = valid_fn(dist)
        for g in range(G):
            for h in range(HPG):
                o_ref[g, :, h * QT:(h + 1) * QT] = jnp.where(valid, _bias_of(dist, rb_ref, g * HPG + h, starts), NEG)

    prompt_table(bc_o, lambda r, i: i - CMP_STRIDE * (r - CMP_PAD) - (CMP_BLOCK - 1), lambda d: d >= 0)
    prompt_table(bs_o, lambda r, i: i + KEY_TILE - r, lambda d: d >= 0)
    prompt_table(bw_o, lambda r, i: i + WINDOW - r, lambda d: (d >= 0) & (d < WINDOW))
    for g in range(G):
        for h in range(HPG):
            bfar_o[g, :, h * QT:(h + 1) * QT] = jnp.full((1, QT), rb_ref[N_BUCKETS - 1, g * HPG + h] * LOG2E, _F32)
    nb = SEL_KEY_CHUNK // SEL_BLOCK
    lane = lax.broadcasted_iota(jnp.int32, (1, LANES), 1)
    bfl_o[...] = jnp.zeros(bfl_o.shape, _F32)
    for g in range(G):
        for h in range(HPG):
            far = jnp.full((1, LANES), rb_ref[N_BUCKETS - 1, g * HPG + h] * LOG2E, _F32)
            hi, lo = _split_hi_lo(far)
            bfl_o[g, h:h + 1, :] = jnp.where(lane == nb, hi.astype(_F32), jnp.where(lane == nb + 1, lo.astype(_F32), 0.0))

    GQ = G * Q
    row = lax.broadcasted_iota(jnp.int32, (GQ, 1), 0)
    row_g = row // Q
    t = PAST_LEN + row % Q

    def by_group(fn):
        out = fn(0)
        for g in range(1, G):
            out = jnp.where(row_g == g, fn(g), out)
        return out

    def sample_table(o_ref, kpos_fn, valid_fn):
        k = lax.broadcasted_iota(jnp.int32, (1, o_ref.shape[1]), 1)
        dist = t - kpos_fn(k)
        valid = valid_fn(dist, k)
        for h in range(HPG):
            b = by_group(lambda g: _bias_of(dist, rb_ref, g * HPG + h, starts))
            o_ref[h * GQ:(h + 1) * GQ, :] = jnp.where(valid, b, NEG)

    w_buf = swin_o.shape[1]
    sample_table(scmp_o, lambda k: CMP_STRIDE * k + CMP_BLOCK - 1, lambda d, k: (d >= 0) & (k < n_cmp_s))
    sample_table(swin_o, lambda k: PAST_LEN - w_buf + k,
                 lambda d, k: (d >= 0) & (d < WINDOW) & (PAST_LEN - w_buf + k >= 0))
    sample_table(slast_o, lambda k: PAST_LEN - PAGE_SIZE + k, lambda d, k: d >= 0)
    sample_table(snew_o, lambda k: PAST_LEN + k, lambda d, k: (d >= 0) & (k < Q))
    for h in range(HPG):
        far = by_group(lambda g: jnp.full((GQ, LANES), rb_ref[N_BUCKETS - 1, g * HPG + h] * LOG2E, _F32))
        sfar_o[h * GQ:(h + 1) * GQ, :] = far


def _bias_tables(rel_bias, n_chunk_s, n_cmp_s, w_buf):
    G, HPG, QT, Q = NSA_KV_GROUPS, HEADS_PER_GROUP, QUERY_TILE, DEC_SEQ
    NL, R = HPG * QT, HPG * G * Q
    f32 = lambda *s: jax.ShapeDtypeStruct(s, _F32)
    return pl.pallas_call(
        functools.partial(_bias_tables_kernel, n_cmp_s=n_cmp_s),
        in_specs=[pl.BlockSpec(memory_space=pltpu.SMEM)],
        out_shape=[f32(G, CMP_WIN, NL), f32(G, 2 * KEY_TILE, NL), f32(G, WINDOW + QT, NL), f32(G, 1, NL),
                   f32(G, SUBLANES, LANES), f32(R, n_chunk_s), f32(R, w_buf), f32(R, PAGE_SIZE), f32(R, PAGE_SIZE), f32(R, LANES)],
        compiler_params=_params(),
        name="bias_tables",
    )(rel_bias)


def kernel(x_prompt, x_sample, cache_cmp, cache_sel, cache_sb, state_win, state_conv, page_table,
           rel_bias, g_pre_mix, w_in, w_ck1, b_ck1, w_ck2, w_cv1, b_cv1, w_cv2, w_branch_a, w_branch_b,
           w_out, g_post_mix, g_pre_ffn, w_ffn_up, conv_w, conv_b, w_ffn_down, g_post_ffn):
    assert DEPTH == 1 and x_prompt.shape[0] == 1
    G, HPG, dh, Q = NSA_KV_GROUPS, HEADS_PER_GROUP, HEAD_DIM, DEC_SEQ
    T = x_prompt.shape[1]
    DB = x_sample.shape[0]
    n_pages = page_table.shape[1]
    w_buf = state_win.shape[2]
    row = lambda v: v.reshape(1, -1)

    w_proj = _split_w_in(w_in[0])
    g_pre = row(g_pre_mix[0])
    w1k, w1v = _bf(_chunk_weights(w_ck1[0])), _bf(_chunk_weights(w_cv1[0]))
    b1k, b1v = row(jnp.tile(b_ck1[0], G)), row(jnp.tile(b_cv1[0], G))
    w2k, w2v = _bf(_group_diag(w_ck2[0])), _bf(_group_diag(w_cv2[0]))
    wa, wb, wo = _bf(w_branch_a[0]), _bf(w_branch_b[0]), _bf(w_out[0])
    wua, wub = _bf(w_ffn_up[0][:, :D_FF]), _bf(w_ffn_up[0][:, D_FF:])
    wd = _bf(w_ffn_down[0])
    cw, cb = conv_w[0], row(conv_b[0])
    g_post_m, g_pre_f, g_post_f = row(g_post_mix[0]), row(g_pre_ffn[0]), row(g_post_ffn[0])
    tt_prompt = jnp.asarray(_cumsum_matrix(latest_first=False), _BF16)
    tt_sample = jnp.asarray(_cumsum_matrix(latest_first=True), _BF16)

    xp = x_prompt[0]
    qa, kv, kvb, gn, qb, sb, sbb, gm = _proj_in(xp, g_pre, w_proj)
    n_chunk = T // CMP_STRIDE
    n_cmp = (T - CMP_BLOCK) // CMP_STRIDE + 1
    kvcol = lambda a, j: a[:, j * KV_WIDTH:(j + 1) * KV_WIDTH]
    kc, vc = _compress_prompt(kvcol(kvb, 0).reshape(n_chunk, CMP_STRIDE * KV_WIDTH),
                              kvcol(kvb, 1).reshape(n_chunk, CMP_STRIDE * KV_WIDTH),
                              w1k, w1v, b1k, b1v, w2k, w2v)
    kcp = jnp.pad(kc, ((CMP_PAD, 0), (0, 0)))
    vcp = jnp.pad(vc, ((CMP_PAD, 0), (0, 0)))
    front = lambda a: jnp.pad(a, ((WINDOW, 0), (0, 0)))
    cpp = PAGE_SIZE // CMP_STRIDE
    n_chunk_s = n_pages * cpp
    n_cmp_s = (PAST_LEN + Q - CMP_BLOCK) // CMP_STRIDE + 1
    bc, bs, bw, bfar, bfl, bcmp, bwin, blast, bnew, bfar_s = _bias_tables(rel_bias, n_chunk_s, n_cmp_s, w_buf)
    ct =jnp.asarray(_cover_matrix(n_chunk + CMP_PAD, T // SEL_BLOCK, n_cmp, CMP_PAD).T, _BF16)
    o_a = _nsa_prompt(qa, gn, kcp, vcp, front(kvcol(kvb, 2)), front(kvcol(kvb, 3)),
                      front(kvcol(kvb, 4)), front(kvcol(kvb, 5)), bc, bs, bw, bfar, bfl, ct)
    o_b = _sb_prompt(qb, sbb, tt_prompt)
    x1, h2 = _merge(xp, o_a, o_b, gm, wa, wb, wo, g_post_m, g_pre_f)
    y_p, tails = _ffn_prompt(h2, x1, wua, wub, cw, cb, wd, g_post_f)

    kv_cmp_p = kv[:, 0:2 * KV_WIDTH].reshape(1, 1, T, 2, G, dh)
    kv_sel_p = kv[:, 2 * KV_WIDTH:4 * KV_WIDTH].reshape(1, 1, T, 2, G, dh)
    kv_sb_p = sb.reshape(1, 1, T, 2, SB_HEADS, dh)
    win_rows = jnp.pad(kv[:, 4 * KV_WIDTH:], ((WINDOW, 0), (0, 0)))[T + WINDOW - w_buf:]
    win_p = win_rows.reshape(1, 1, w_buf, 2, G, dh)
    conv_p = tails[-(CONV_W - 1):].reshape(1, 1, CONV_W - 1, D_FF)

    xs = jnp.transpose(x_sample, (1, 0, 2)).reshape(Q * DB, D_MODEL)
    qa_s, kv_s, _, gn_s, qb_s, sb_s, _, gm_s = _proj_in(xs, g_pre, w_proj)
    by_seq = lambda a: jnp.transpose(a.reshape(Q, DB, -1), (1, 0, 2))
    page_flat = page_table.reshape(-1)

    assert n_cmp_s < n_chunk_s + 1 and Q <= SUBLANES and Q <= CMP_STRIDE
    parts = _cmp_pages(page_flat, _pages_last(cache_cmp[0]), _bf(_page_chunk_weights(w_ck1[0], w_cv1[0])))
    parts = parts.reshape(DB, n_chunk_s, 4 * KV_WIDTH)

    q5 = by_seq(qa_s).reshape(DB, Q, G, HPG, dh)
    q5 = jnp.transpose(q5, (0, 3, 2, 1, 4))
    q32 = jnp.einsum('bhgqd,gk->bhgqkd', q5, jnp.eye(G, dtype=q5.dtype)).reshape(DB, HPG * G * Q, KV_WIDTH)
    g5 = by_seq(gn_s)[:, :, :3 * NSA_HEADS].reshape(DB, Q, 3, G, HPG)
    gate32 = jnp.transpose(g5, (0, 4, 3, 1, 2)).reshape(DB, HPG * G * Q, 3)
    gate32 = jnp.pad(gate32, ((0, 0), (0, 0), (0, LANES - 3)))
    new8 = jnp.pad(by_seq(kv_s), ((0, 0), (0, SUBLANES - Q), (0, 0)))
    n_sel_s = -(-(PAST_LEN + Q) // SEL_BLOCK)
    n_blk_s = -(-n_sel_s // LANES) * LANES
    cmat =jnp.asarray(_cover_matrix(n_chunk_s, n_blk_s, n_cmp_s, 0), _BF16)
    cmat = (cmat * (np.arange(n_blk_s)[None, :] < n_sel_s)).T
    cur_blocks = tuple(int((PAST_LEN + j) // SEL_BLOCK) for j in range(Q))
    o32 = _nsa_sample(page_flat, _pages_last(cache_sel[0]), q32, gate32, parts,
                      _pages_last(state_win[0]), new8, b1k, b1v, w2k, w2v,
                      bcmp, bwin, blast, bnew, bfar_s, cmat, n_cmp_s, n_sel_s, cur_blocks)
    o6 = o32.reshape(DB, HPG, G, Q, G, dh)
    o_a_s = jnp.stack([o6[:, :, g, :, g, :] for g in range(G)], axis=1)
    o_a_s = jnp.transpose(o_a_s, (3, 0, 1, 2, 4)).reshape(Q * DB, NSA_WIDTH)

    qb5 = jnp.transpose(by_seq(qb_s).reshape(DB, Q, SB_HEADS, dh), (0, 2, 1, 3))
    qsb = jnp.einsum('bhqd,hk->bhqkd', qb5, jnp.eye(SB_HEADS, dtype=qb5.dtype)).reshape(DB, SB_HEADS * Q, SB_WIDTH)
    sb_new8 = jnp.pad(by_seq(sb_s), ((0, 0), (0, SUBLANES - Q), (0, 0)))
    rq = np.arange(SB_HEADS * Q)[:, None] % Q
    nmask = jnp.asarray((np.arange(PAGE_SIZE)[None, :] < rq).astype(np.float32))
    osb = _sb_sample(page_flat, _pages_last(cache_sb[0]), qsb, sb_new8, nmask, tt_sample)
    o7 = osb.reshape(DB, SB_HEADS, Q, SB_HEADS, dh)
    o_b_s = jnp.stack([o7[:, h, :, h, :] for h in range(SB_HEADS)], axis=1)
    o_b_s = jnp.transpose(o_b_s, (2, 0, 1, 3)).reshape(Q * DB, SB_WIDTH)

    x1_s, h2_s = _merge(xs, _bf(o_a_s), _bf(o_b_s), gm_s, wa, wb, wo, g_post_m, g_pre_f)
    sc = jnp.transpose(state_conv[0], (1, 0, 2))
    y_s_rows, tail_s = _ffn_sample(h2_s, x1_s, sc, wua, wub, cw, cb, wd, g_post_f)

    y_s = by_seq(y_s_rows)
    kv_seq = by_seq(kv_s)
    kv_cmp_s = kv_seq[:, :, 0:2 * KV_WIDTH].reshape(1, DB, Q, 2, G, dh)
    kv_sel_s = kv_seq[:, :, 2 * KV_WIDTH:4 * KV_WIDTH].reshape(1, DB, Q, 2, G, dh)
    kv_sb_s = by_seq(sb_s).reshape(1, DB, Q, 2, SB_HEADS, dh)
    win_new = kv_seq[:, :, 4 * KV_WIDTH:].reshape(DB, Q, 2, G, dh)
    win_s = jnp.concatenate([state_win[0], win_new], axis=1)[:, -w_buf:][None]
    conv_s = jnp.transpose(tail_s.reshape(CONV_W - 1, DB, D_FF), (1, 0, 2))[None]

    return (y_p[None], y_s, kv_cmp_p, kv_sel_p, kv_sb_p, win_p, conv_p,
            kv_cmp_s, kv_sel_s, kv_sb_s, win_s, conv_s)
```
